```python
import math
import jax
import jax.numpy as jnp
from jax import lax
import numpy as np

D_MODEL = 1024
BATCH = 4
SEQ = 4096
DEPTH = 4
DEC_BATCH = 128
DEC_SEQ = 4
PAST_LEN = 2048
PAGE_SIZE = 128

POOL_W = D_MODEL // 2
POOL_WINDOWS = (2, 4, 8, 16)
POOL_GROUPS = len(POOL_WINDOWS)
POOL_GROUP = POOL_W // POOL_GROUPS
POOL_STATE = max(POOL_WINDOWS) - 1
MOBA_HEADS = 8
MOBA_HEAD_DIM = 64
ATT_W = MOBA_HEADS * MOBA_HEAD_DIM
MOBA_BLOCK = 256
MOBA_TOPK = 3
QUERY_CHUNK = 128
X_HEADS = 4
X_HEAD_DIM = 128
X_W = X_HEADS * X_HEAD_DIM
MEM_LEN = 256
N_BUCKETS = 32
MAX_EXACT = N_BUCKETS // 2
MAX_DISTANCE = 128
D_FF = -(-8 * D_MODEL // (3 * 256)) * 256
EPS = 1e-6
IN_COLS = POOL_W + 3 * ATT_W + X_W + 3 * D_MODEL
SPLITS = (POOL_W, POOL_W + ATT_W, POOL_W + 2 * ATT_W, POOL_W + 3 * ATT_W,
          POOL_W + 3 * ATT_W + X_W, POOL_W + 3 * ATT_W + X_W + D_MODEL,
          POOL_W + 3 * ATT_W + X_W + 2 * D_MODEL)

kernel_name = 'pool_moba_memory_hybrid_step'


def rmsnorm(x, g):
    xf = x.astype(jnp.float32)
    xf = xf * lax.rsqrt(jnp.mean(xf * xf, axis=-1, keepdims=True) + EPS)
    return (xf * g.astype(jnp.float32)).astype(x.dtype)


def t5_bucket(rel):
    n = jnp.maximum(rel, 0)
    nf = jnp.maximum(n, 1).astype(jnp.float32)
    large = MAX_EXACT + (jnp.log(nf / MAX_EXACT) / math.log(MAX_DISTANCE / MAX_EXACT)
                         * (N_BUCKETS - MAX_EXACT)).astype(jnp.int32)
    large = jnp.minimum(large, N_BUCKETS - 1)
    return jnp.where(n < MAX_EXACT, n, large)


def pool_mixer(u, buf, pos, pool_w, pool_scale):
    b, t, c = u.shape
    padded = jnp.concatenate([buf, u], axis=1)
    csum = jnp.concatenate([jnp.zeros((b, 1, c), jnp.float32),
                            jnp.cumsum(padded.astype(jnp.float32), axis=1)], axis=1)
    s0 = POOL_STATE + 1
    uf = u.astype(jnp.float32)
    groups = []
    for g, w in enumerate(POOL_WINDOWS):
        sl = slice(g * POOL_GROUP, (g + 1) * POOL_GROUP)
        win_sum = csum[:, s0:s0 + t, sl] - csum[:, s0 - w:s0 - w + t, sl]
        count = jnp.minimum(pos + 1, w).astype(jnp.float32)[None, :, None]
        groups.append(win_sum / count - uf[:, :, sl])
    pooled = jnp.stack(groups, axis=2)
    mixed = jnp.einsum('btgc,gcd->btgd', pooled, pool_w.astype(jnp.float32)).reshape(b, t, c)
    mixed = mixed * pool_scale.astype(jnp.float32)
    new_buf = padded[:, -POOL_STATE:]
    return mixed.astype(u.dtype), new_buf


def moba_attention(q, k, v, q_pos, rel_bias):
    b, tq, h, dh = q.shape
    l = k.shape[1]
    nb = max(-(-l // MOBA_BLOCK), MOBA_TOPK)
    pad = ((0, 0), (0, nb * MOBA_BLOCK - l), (0, 0), (0, 0))
    kb = jnp.pad(k, pad).reshape(b, nb, MOBA_BLOCK, h, dh).transpose(0, 3, 1, 2, 4)
    vb = jnp.pad(v, pad).reshape(b, nb, MOBA_BLOCK, h, dh).transpose(0, 3, 1, 2, 4)
    kmean = jnp.mean(kb.astype(jnp.float32), axis=3)
    qc = QUERY_CHUNK if tq % QUERY_CHUNK == 0 else tq
    nq = tq // qc
    q_blocks = q.reshape(b, nq, qc, h, dh)
    pos_blocks = q_pos.reshape(nq, qc)
    bias_hb = rel_bias.T
    h_idx = jnp.arange(h)[None, :, None]
    blk_ids = jnp.arange(nb)
    scale = 1.0 / math.sqrt(dh)

    def attend_chunk(kb_s, vb_s, km_s, qch, pch):
        qf = qch.astype(jnp.float32)
        b_t = pch // MOBA_BLOCK
        s = jnp.einsum('thd,hjd->thj', qf, km_s)
        s = jnp.where(blk_ids[None, None, :] < b_t[:, None, None], s, -jnp.inf)
        _, top_idx = lax.top_k(s, MOBA_TOPK)
        top_ok = jnp.arange(MOBA_TOPK)[None, None, :] < b_t[:, None, None]
        sel = jnp.concatenate([top_idx.astype(jnp.int32),
                               jnp.broadcast_to(b_t[:, None, None], (qc, h, 1))], axis=-1)
        sel_ok = jnp.concatenate([jnp.broadcast_to(top_ok, (qc, h, MOBA_TOPK)),
                                  jnp.ones((qc, h, 1), bool)], axis=-1)
        kg = kb_s[h_idx, sel].astype(jnp.float32)
        vg = vb_s[h_idx, sel].astype(jnp.float32)
        k_pos = sel[..., None] * MOBA_BLOCK + jnp.arange(MOBA_BLOCK)
        rel = pch[:, None, None, None] - k_pos
        bias = bias_hb[h_idx[..., None], t5_bucket(rel)].astype(jnp.float32)
        logits = jnp.einsum('thd,thnkd->thnk', qf, kg) * scale + bias
        logits = jnp.where(sel_ok[..., None] & (rel >= 0), logits, -jnp.inf)
        p = jax.nn.softmax(logits.reshape(qc, h, -1), axis=-1)
        o = jnp.einsum('thk,thkd->thd', p, vg.reshape(qc, h, -1, dh))
        return o.astype(q.dtype)

    def attend_seq(args):
        q_s, kb_s, vb_s, km_s = args
        return lax.map(lambda a: attend_chunk(kb_s, vb_s, km_s, a[0], a[1]), (q_s, pos_blocks))

    out = lax.map(attend_seq, (q_blocks, kb, vb, kmean))
    return out.reshape(b, tq, h * dh)


def cross_attend(q, mem_k, mem_v):
    b, t = q.shape[:2]
    logits = jnp.einsum('bthd,bmhd->bhtm', q.astype(jnp.float32),
                        mem_k.astype(jnp.float32)) * (1.0 / math.sqrt(X_HEAD_DIM))
    p = jax.nn.softmax(logits, axis=-1)
    o = jnp.einsum('bhtm,bmhd->bthd', p, mem_v.astype(jnp.float32))
    return o.reshape(b, t, X_W).astype(q.dtype)


def memory_kv(mem, g, w):
    b, m = mem.shape[:2]
    mk, mv = jnp.split(rmsnorm(mem, g) @ w, 2, axis=-1)
    return (mk.reshape(b, m, X_HEADS, X_HEAD_DIM), mv.reshape(b, m, X_HEADS, X_HEAD_DIM))


def decoder_layer(x, mem_k, mem_v, pool_buf, k_past, v_past, pos, norm1, w_in, pool_w, pool_scale,
                  w_br_a, w_br_b, w_br_c, w_out, norm2, w_ffn_in, w_ffn_out, rel_bias):
    b, t, _ = x.shape
    xn = rmsnorm(x, norm1)
    u, q, k, v, qx, ga, gb, gc = jnp.split(xn @ w_in, SPLITS, axis=-1)
    a_out, new_buf = pool_mixer(u, pool_buf, pos, pool_w, pool_scale)
    kh = k.reshape(b, t, MOBA_HEADS, MOBA_HEAD_DIM)
    vh = v.reshape(b, t, MOBA_HEADS, MOBA_HEAD_DIM)
    if k_past is None:
        k_all, v_all = kh, vh
    else:
        k_all = jnp.concatenate([k_past.astype(kh.dtype), kh], axis=1)
        v_all = jnp.concatenate([v_past.astype(vh.dtype), vh], axis=1)
    b_out = moba_attention(q.reshape(b, t, MOBA_HEADS, MOBA_HEAD_DIM), k_all, v_all, pos, rel_bias)
    c_out = cross_attend(qx.reshape(b, t, X_HEADS, X_HEAD_DIM), mem_k, mem_v)
    merged = (jax.nn.sigmoid(ga) * (a_out @ w_br_a)
              + jax.nn.sigmoid(gb) * (b_out @ w_br_b)
              + jax.nn.sigmoid(gc) * (c_out @ w_br_c))
    x = x + merged @ w_out
    gate, up = jnp.split(rmsnorm(x, norm2) @ w_ffn_in, 2, axis=-1)
    x = x + (jax.nn.silu(gate) * up) @ w_ffn_out
    return x, kh, vh, new_buf


def setup_inputs(seed: int = 0) -> dict:
    key = jax.random.key(seed)
    ks = jax.random.split(key, 24)
    f32 = jnp.float32
    d = D_MODEL
    n_pages = PAST_LEN // PAGE_SIZE
    n_used = DEC_BATCH * n_pages
    n_phys = n_used + max(1, n_used // 4)

    def nrm(k, shape, scale=1.0):
        return jax.random.normal(k, shape, f32) * scale

    page_table = jax.random.permutation(ks[7], n_phys)[:n_used].reshape(DEC_BATCH, n_pages).astype(jnp.int32)
    return {
        'x_prompt': nrm(ks[0], (BATCH, SEQ, d)),
        'x_sample': nrm(ks[1], (DEC_BATCH, DEC_SEQ, d)),
        'cache_attn_k': nrm(ks[2], (DEPTH, n_phys, PAGE_SIZE, MOBA_HEADS, MOBA_HEAD_DIM)),
        'cache_attn_v': nrm(ks[3], (DEPTH, n_phys, PAGE_SIZE, MOBA_HEADS, MOBA_HEAD_DIM)),
        'cache_mem_k': nrm(ks[4], (DEPTH, DEC_BATCH, MEM_LEN, X_HEADS, X_HEAD_DIM)),
        'cache_mem_v': nrm(ks[5], (DEPTH, DEC_BATCH, MEM_LEN, X_HEADS, X_HEAD_DIM)),
        'state_pool': nrm(ks[6], (DEPTH, DEC_BATCH, POOL_STATE, POOL_W)),
        'page_table': page_table,
        'mem_prompt': nrm(ks[8], (BATCH, MEM_LEN, d)),
        'norm1': 1.0 + nrm(ks[9], (DEPTH, d), 0.05),
        'w_in': nrm(ks[10], (DEPTH, d, IN_COLS), d ** -0.5),
        'pool_w': nrm(ks[11], (DEPTH, POOL_GROUPS, POOL_GROUP, POOL_GROUP), POOL_GROUP ** -0.5),
        'pool_scale': 1.0 + nrm(ks[12], (DEPTH, POOL_W), 0.1),
        'mem_norm': 1.0 + nrm(ks[13], (DEPTH, d), 0.05),
        'w_mem_kv': nrm(ks[14], (DEPTH, d, 2 * X_W), d ** -0.5),
        'w_br_a': nrm(ks[15], (DEPTH, POOL_W, d), POOL_W ** -0.5),
        'w_br_b': nrm(ks[16], (DEPTH, ATT_W, d), ATT_W ** -0.5),
        'w_br_c': nrm(ks[17], (DEPTH, X_W, d), X_W ** -0.5),
        'w_out': nrm(ks[18], (DEPTH, d, d), d ** -0.5),
        'norm2': 1.0 + nrm(ks[19], (DEPTH, d), 0.05),
        'w_ffn_in': nrm(ks[20], (DEPTH, d, 2 * D_FF), d ** -0.5),
        'w_ffn_out': nrm(ks[21], (DEPTH, D_FF, d), D_FF ** -0.5),
        'rel_bias': nrm(ks[22], (N_BUCKETS, MOBA_HEADS), 0.5),
        'final_norm': 1.0 + nrm(ks[23], (d,), 0.05),
    }


def reference(x_prompt, x_sample, cache_attn_k, cache_attn_v, cache_mem_k, cache_mem_v, state_pool,
              page_table, mem_prompt, norm1, w_in, pool_w, pool_scale, mem_norm, w_mem_kv,
              w_br_a, w_br_b, w_br_c, w_out, norm2, w_ffn_in, w_ffn_out, rel_bias, final_norm):
    n_prompt, seq = x_prompt.shape[:2]
    n_sample, dec_seq = x_sample.shape[:2]
    past_len = page_table.shape[1] * cache_attn_k.shape[2]
    pos_p = jnp.arange(seq, dtype=jnp.int32)
    pos_s = past_len + jnp.arange(dec_seq, dtype=jnp.int32)
    buf_p = jnp.zeros((n_prompt, POOL_STATE, POOL_W), x_prompt.dtype)
    hp, hs = x_prompt, x_sample
    kp_l, vp_l, mkp_l, mvp_l, bp_l, ks_l, vs_l, bs_l = [], [], [], [], [], [], [], []
    for l in range(DEPTH):
        lw = (norm1[l], w_in[l], pool_w[l], pool_scale[l], w_br_a[l], w_br_b[l], w_br_c[l],
              w_out[l], norm2[l], w_ffn_in[l], w_ffn_out[l], rel_bias)
        mk_p, mv_p = memory_kv(mem_prompt, mem_norm[l], w_mem_kv[l])
        hp, kp, vp, bp = decoder_layer(hp, mk_p, mv_p, buf_p, None, None, pos_p, *lw)
        k_past = cache_attn_k[l, page_table].reshape(n_sample, past_len, MOBA_HEADS, MOBA_HEAD_DIM)
        v_past = cache_attn_v[l, page_table].reshape(n_sample, past_len, MOBA_HEADS, MOBA_HEAD_DIM)
        hs, ks_, vs_, bs = decoder_layer(hs, cache_mem_k[l], cache_mem_v[l], state_pool[l],
                                         k_past, v_past, pos_s, *lw)
        kp_l.append(kp)
        vp_l.append(vp)
        mkp_l.append(mk_p)
        mvp_l.append(mv_p)
        bp_l.append(bp)
        ks_l.append(ks_)
        vs_l.append(vs_)
        bs_l.append(bs)
    y_prompt = rmsnorm(hp, final_norm)
    y_sample = rmsnorm(hs, final_norm)
    return (y_prompt, y_sample, jnp.stack(kp_l), jnp.stack(vp_l), jnp.stack(mkp_l), jnp.stack(mvp_l),
            jnp.stack(bp_l), jnp.stack(ks_l), jnp.stack(vs_l), jnp.stack(bs_l))
```

```python
import functools
import math

import jax
import jax.numpy as jnp
from jax import lax
from jax.experimental import pallas as pl
from jax.experimental.pallas import tpu as pltpu

F32 = jnp.float32
BF16 = jnp.bfloat16

D_MODEL = 1024
POOL_W = 512
POOL_WINDOWS = (2, 4, 8, 16)
POOL_GROUP = POOL_W // len(POOL_WINDOWS)
POOL_STATE = max(POOL_WINDOWS) - 1
MOBA_HEADS = 8
MOBA_HEAD_DIM = 64
ATT_W = MOBA_HEADS * MOBA_HEAD_DIM
MOBA_BLOCK = 256
MOBA_TOPK = 3
X_HEADS = 4
X_HEAD_DIM = 128
X_W = X_HEADS * X_HEAD_DIM
N_BUCKETS = 32
MAX_EXACT = N_BUCKETS // 2
MAX_DISTANCE = 128
EPS = 1e-6
PAGE = 128

ROW_TILE = 256
AUG_W = 128
MASKED = -1e30
MIB = 1 << 20


def _params(n_axes, vmem_mib):
    return pltpu.CompilerParams(dimension_semantics=("arbitrary",) * n_axes,
                                vmem_limit_bytes=vmem_mib * MIB)


def _rmsnorm(x, g):
    return x * lax.rsqrt(jnp.mean(x * x, axis=-1, keepdims=True) + EPS) * g


def _bdot(a, b):
    return jnp.dot(a.astype(BF16), b.astype(BF16), preferred_element_type=F32)


def _bdot_nt(a, b):
    return lax.dot_general(a.astype(BF16), b.astype(BF16), (((1,), (1,)), ((), ())),
                           preferred_element_type=F32)


def _bucket(rel):
    n = jnp.maximum(rel, 0)
    nf = jnp.maximum(n, 1).astype(F32)
    large = MAX_EXACT + (jnp.log(nf / MAX_EXACT) / math.log(MAX_DISTANCE / MAX_EXACT)
                         * (N_BUCKETS - MAX_EXACT)).astype(jnp.int32)
    large = jnp.minimum(large, N_BUCKETS - 1)
    return jnp.where(n < MAX_EXACT, n, large)


def _bias_lookup(rel, rb_ref, h):
    bucket = _bucket(rel)
    last = rb_ref[N_BUCKETS - 1, h]
    out = jnp.zeros(rel.shape, F32)
    for b in range(N_BUCKETS - 1):
        out = jnp.where(bucket == b, rb_ref[b, h] - last, out)
    return out


def _bias_table_kernel(rb_ref, own_ref, prev_ref, spast_ref, sown_ref, *, past_len):
    h = pl.program_id(0)
    blk = MOBA_BLOCK
    tk = lax.broadcasted_iota(jnp.int32, (blk, blk), 0)
    tq = lax.broadcasted_iota(jnp.int32, (blk, blk), 1)
    rel = tq - tk
    own_ref[0] = jnp.where(rel >= 0, _bias_lookup(rel, rb_ref, h), MASKED)
    prev_ref[0] = _bias_lookup(rel + blk, rb_ref, h)
    t = lax.broadcasted_iota(jnp.int32, (8, past_len), 0)
    key = lax.broadcasted_iota(jnp.int32, (8, past_len), 1)
    spast_ref[0] = _bias_lookup(past_len + t - key, rb_ref, h)
    t = lax.broadcasted_iota(jnp.int32, (8, 128), 0)
    key = lax.broadcasted_iota(jnp.int32, (8, 128), 1)
    sown_ref[0] = jnp.where(key <= t, _bias_lookup(t - key, rb_ref, h), MASKED)


def _bias_tables(rel_bias, past_len):
    blk = MOBA_BLOCK
    return pl.pallas_call(
        functools.partial(_bias_table_kernel, past_len=past_len),
        grid=(MOBA_HEADS,),
        in_specs=[pl.BlockSpec(memory_space=pltpu.SMEM)],
        out_specs=[pl.BlockSpec((1, blk, blk), lambda h: (h, 0, 0)),
                   pl.BlockSpec((1, blk, blk), lambda h: (h, 0, 0)),
                   pl.BlockSpec((1, 8, past_len), lambda h: (h, 0, 0)),
                   pl.BlockSpec((1, 8, 128), lambda h: (h, 0, 0))],
        out_shape=[jax.ShapeDtypeStruct((MOBA_HEADS, blk, blk), F32),
                   jax.ShapeDtypeStruct((MOBA_HEADS, blk, blk), F32),
                   jax.ShapeDtypeStruct((MOBA_HEADS, 8, past_len), F32),
                   jax.ShapeDtypeStruct((MOBA_HEADS, 8, 128), F32)],
        compiler_params=_params(1, 32),
        name="bias_tables",
    )(rel_bias)


def _in_proj_kernel(x_ref, g_ref, w_ref, u_ref, q_ref, k_ref, v_ref, qx_ref, gates_ref, kmean_ref):
    xb = _rmsnorm(x_ref[...], g_ref[...]).astype(BF16)

    def proj(lo, n):
        return jnp.dot(xb, w_ref[:, lo:lo + n], preferred_element_type=F32)

    u_ref[...] = proj(0, POOL_W)
    q_ref[...] = proj(POOL_W, ATT_W)
    k = proj(POOL_W + ATT_W, ATT_W)
    k_ref[...] = k
    for r in range(ROW_TILE // MOBA_BLOCK):
        kmean_ref[r] = jnp.mean(k[r * MOBA_BLOCK:(r + 1) * MOBA_BLOCK], axis=0, keepdims=True)
    v_ref[...] = proj(POOL_W + 2 * ATT_W, ATT_W)
    qx_ref[...] = proj(POOL_W + 3 * ATT_W, X_W)
    gates_ref[...] = jax.nn.sigmoid(proj(POOL_W + 3 * ATT_W + X_W, 3 * D_MODEL))


def _in_proj(x, norm1, w_in, layer):
    n = x.shape[0]
    in_cols = w_in.shape[-1]
    row = lambda w: pl.BlockSpec((ROW_TILE, w), lambda i: (i, 0))
    blocks_per_tile = ROW_TILE // MOBA_BLOCK
    return pl.pallas_call(
        _in_proj_kernel,
        grid=(n // ROW_TILE,),
        in_specs=[row(D_MODEL),
                  pl.BlockSpec((None, 1, D_MODEL), lambda i: (layer, 0, 0)),
                  pl.BlockSpec((None, D_MODEL, in_cols), lambda i: (layer, 0, 0))],
        out_specs=[row(POOL_W), row(ATT_W), row(ATT_W), row(ATT_W), row(X_W), row(3 * D_MODEL),
                   pl.BlockSpec((blocks_per_tile, 1, ATT_W), lambda i: (i, 0, 0))],
        out_shape=[jax.ShapeDtypeStruct((n, POOL_W), F32), jax.ShapeDtypeStruct((n, ATT_W), F32),
                   jax.ShapeDtypeStruct((n, ATT_W), F32), jax.ShapeDtypeStruct((n, ATT_W), F32),
                   jax.ShapeDtypeStruct((n, X_W), F32), jax.ShapeDtypeStruct((n, 3 * D_MODEL), F32),
                   jax.ShapeDtypeStruct((n // MOBA_BLOCK, 1, ATT_W), F32)],
        compiler_params=_params(1, 48),
        name="in_proj",
    )(x, norm1, w_in)


def _mem_kv_kernel(x_ref, g_ref, w_ref, mk_ref, mv_ref):
    xb = _rmsnorm(x_ref[...], g_ref[...]).astype(BF16)
    mk_ref[...] = jnp.dot(xb, w_ref[:, :X_W], preferred_element_type=F32)
    mv_ref[...] = jnp.dot(xb, w_ref[:, X_W:], preferred_element_type=F32)


def _mem_kv(mem, mem_norm, w_mem_kv, layer):
    n = mem.shape[0]
    row = lambda w: pl.BlockSpec((ROW_TILE, w), lambda i: (i, 0))
    return pl.pallas_call(
        _mem_kv_kernel,
        grid=(n // ROW_TILE,),
        in_specs=[row(D_MODEL),
                  pl.BlockSpec((None, 1, D_MODEL), lambda i: (layer, 0, 0)),
                  pl.BlockSpec((None, D_MODEL, 2 * X_W), lambda i: (layer, 0, 0))],
        out_specs=[row(X_W), row(X_W)],
        out_shape=[jax.ShapeDtypeStruct((n, X_W), F32), jax.ShapeDtypeStruct((n, X_W), F32)],
        compiler_params=_params(1, 32),
        name="mem_kv",
    )(mem, mem_norm, w_mem_kv)


def _gate_augment_kernel(q_ref, k_ref, v_ref, kmt_ref, qa_ref, ka_ref, vt_ref, sel_ref, *, nb):
    i = pl.program_id(1)
    blk = MOBA_BLOCK
    qt = q_ref[...].T
    st = jnp.dot(kmt_ref[0], qt, precision=lax.Precision.HIGHEST,
                 preferred_element_type=F32)
    row_block = lax.broadcasted_iota(jnp.int32, st.shape, 0) // MOBA_HEADS
    st = jnp.where(row_block < i, st, -jnp.inf)
    slabs = [st[j * MOBA_HEADS:(j + 1) * MOBA_HEADS] for j in range(nb)]
    for j in range(nb):
        rank = jnp.zeros((MOBA_HEADS, blk), jnp.int32)
        for jp in range(nb):
            if jp == j:
                continue
            ahead = (slabs[jp] >= slabs[j]) if jp < j else (slabs[jp] > slabs[j])
            rank = rank + ahead.astype(jnp.int32)
        limit = jnp.where(j < i, MOBA_TOPK, jnp.where(j == i, nb + 1, 0))
        mask = jnp.where(rank < limit, 0.0, MASKED)
        for c in range(blk // 128):
            sel_ref[c, j * MOBA_HEADS:(j + 1) * MOBA_HEADS, :] = mask[:, c * 128:(c + 1) * 128]

    pad_rows = AUG_W - MOBA_HEAD_DIM - nb
    for h in range(MOBA_HEADS):
        pieces = [qt[h * MOBA_HEAD_DIM:(h + 1) * MOBA_HEAD_DIM] * (MOBA_HEAD_DIM ** -0.5),
                  jnp.concatenate([sel_ref[c, pl.ds(h, nb, stride=MOBA_HEADS), :]
                                   for c in range(blk // 128)], axis=1)]
        if pad_rows:
            pieces.append(jnp.zeros((pad_rows, blk), F32))
        qa_ref[0, h] = jnp.concatenate(pieces, axis=0).astype(BF16)

    lane = lax.broadcasted_iota(jnp.int32, (blk, AUG_W), 1)
    block_col = jnp.where(lane == MOBA_HEAD_DIM + i, 1.0, 0.0)
    k = k_ref[...]
    for pair in range(MOBA_HEADS // 2):
        x = k[:, pair * AUG_W:(pair + 1) * AUG_W]
        ka_ref[0, 2 * pair, 0] = jnp.where(lane < MOBA_HEAD_DIM, x, block_col).astype(BF16)
        ka_ref[0, 2 * pair + 1, 0] = jnp.where(lane < MOBA_HEAD_DIM, pltpu.roll(x, MOBA_HEAD_DIM, 1),
                                               block_col).astype(BF16)
    vt = v_ref[...].T
    for h in range(MOBA_HEADS):
        vt_ref[0, h, 0] = vt[h * MOBA_HEAD_DIM:(h + 1) * MOBA_HEAD_DIM].astype(BF16)


def _gate_augment(q, k, v, kmt, batch, seq):
    nb = seq // MOBA_BLOCK
    blk = MOBA_BLOCK
    row = pl.BlockSpec((blk, ATT_W), lambda b, i: (b * nb + i, 0))
    return pl.pallas_call(
        functools.partial(_gate_augment_kernel, nb=nb),
        grid=(batch, nb),
        in_specs=[row, row, row, pl.BlockSpec((1, nb * MOBA_HEADS, ATT_W), lambda b, i: (b, 0, 0))],
        out_specs=[pl.BlockSpec((1, MOBA_HEADS, AUG_W, blk), lambda b, i: (b, 0, 0, i)),
                   pl.BlockSpec((1, MOBA_HEADS, 1, blk, AUG_W), lambda b, i: (b, 0, i, 0, 0)),
                   pl.BlockSpec((1, MOBA_HEADS, 1, MOBA_HEAD_DIM, blk), lambda b, i: (b, 0, i, 0, 0))],
        out_shape=[jax.ShapeDtypeStruct((batch, MOBA_HEADS, AUG_W, seq), BF16),
                   jax.ShapeDtypeStruct((batch, MOBA_HEADS, nb, blk, AUG_W), BF16),
                   jax.ShapeDtypeStruct((batch, MOBA_HEADS, nb, MOBA_HEAD_DIM, blk), BF16)],
        scratch_shapes=[pltpu.VMEM((blk // 128, nb * MOBA_HEADS, 128), F32)],
        compiler_params=_params(2, 32),
        name="gate_augment",
    )(q, k, v, kmt)


def _moba_kernel(qa_ref, ka_ref, vt_ref, own_ref, prev_ref, o_ref):
    i = pl.program_id(1)

    def scores(h, j):
        return jnp.dot(ka_ref[0, h, j], qa_ref[0, h], preferred_element_type=F32)

    def fold(carry, s, h, j):
        m, l, acc = carry
        m_new = jnp.maximum(m, jnp.max(s, axis=0, keepdims=True))
        alpha = jnp.exp(m - m_new)
        p = jnp.exp(s - m_new)
        l = alpha * l + jnp.sum(p, axis=0, keepdims=True)
        acc = alpha * acc + jnp.dot(vt_ref[0, h, j], p.astype(BF16), preferred_element_type=F32)
        return m_new, l, acc

    for pair in range(MOBA_HEADS // 2):
        outs = []
        for h in (2 * pair, 2 * pair + 1):
            s = scores(h, i) + own_ref[h]
            m = jnp.max(s, axis=0, keepdims=True)
            p = jnp.exp(s - m)
            l = jnp.sum(p, axis=0, keepdims=True)
            acc = jnp.dot(vt_ref[0, h, i], p.astype(BF16), preferred_element_type=F32)
            carry = lax.fori_loop(0, i - 1, lambda j, c, h=h: fold(c, scores(h, j), h, j), (m, l, acc))
            carry = lax.cond(i >= 1,
                             lambda c, h=h: fold(c, scores(h, i - 1) + prev_ref[h], h, i - 1),
                             lambda c: c, carry)
            m, l, acc = carry
            outs.append(acc * (1.0 / l))
        o_ref[:, pair * AUG_W:(pair + 1) * AUG_W] = jnp.concatenate(outs, axis=0).T


def _moba_prompt(qa, ka, vt, bias_own, bias_prev, batch, seq):
    nb = seq // MOBA_BLOCK
    blk = MOBA_BLOCK
    return pl.pallas_call(
        _moba_kernel,
        grid=(batch, nb),
        in_specs=[pl.BlockSpec((1, MOBA_HEADS, AUG_W, blk), lambda b, i: (b, 0, 0, i)),
                  pl.BlockSpec((1, MOBA_HEADS, nb, blk, AUG_W), lambda b, i: (b, 0, 0, 0, 0)),
                  pl.BlockSpec((1, MOBA_HEADS, nb, MOBA_HEAD_DIM, blk), lambda b, i: (b, 0, 0, 0, 0)),
                  pl.BlockSpec((MOBA_HEADS, blk, blk), lambda b, i: (0, 0, 0)),
                  pl.BlockSpec((MOBA_HEADS, blk, blk), lambda b, i: (0, 0, 0))],
        out_specs=pl.BlockSpec((blk, ATT_W), lambda b, i: (b * nb + i, 0)),
        out_shape=jax.ShapeDtypeStruct((batch * seq, ATT_W), F32),
        compiler_params=_params(2, 48),
        name="moba_prompt",
    )(qa, ka, vt, bias_own, bias_prev)


def _window_means(pad_ref, base, rows, inv_count):
    groups = []
    for g, w in enumerate(POOL_WINDOWS):
        cols = slice(g * POOL_GROUP, (g + 1) * POOL_GROUP)
        tok = pad_ref[base:base + rows, cols]
        acc = tok
        for s in range(1, w):
            acc = acc + pad_ref[base - s:base - s + rows, cols]
        groups.append(acc * inv_count(w) - tok)
    return jnp.concatenate(groups, axis=1)


def _cross_attend(qx, mk, mv):
    outs = []
    for h in range(X_HEADS):
        cols = slice(h * X_HEAD_DIM, (h + 1) * X_HEAD_DIM)
        s = _bdot_nt(qx[:, cols], mk[:, cols]) * (X_HEAD_DIM ** -0.5)
        m = jnp.max(s, axis=-1, keepdims=True)
        p = jnp.exp(s - m)
        l = jnp.sum(p, axis=-1, keepdims=True)
        outs.append(_bdot(p, mv[:, cols]) * (1.0 / l))
    return jnp.concatenate(outs, axis=1)


def _prompt_side_kernel(u_ref, halo_ref, qx_ref, mk_ref, mv_ref, pooled_ref, c_ref, pad_ref):
    i = pl.program_id(1)
    rows = ROW_TILE
    base = 16
    first = jnp.full((base, 1), i, jnp.int32) == 0
    pad_ref[0:base, :] = jnp.where(first, 0.0, halo_ref[...])
    pad_ref[base:base + rows, :] = u_ref[...]
    pos = i * rows + lax.broadcasted_iota(jnp.int32, (rows, 1), 0)

    def inv_count(w):
        return 1.0 / jnp.minimum(pos + 1, w).astype(F32)

    pooled_ref[...] = _window_means(pad_ref, base, rows, inv_count)
    c_ref[...] = _cross_attend(qx_ref[...], mk_ref[...].astype(BF16), mv_ref[...].astype(BF16))


def _prompt_side(u, qx, mk, mv, batch, seq, mem_len):
    n = batch * seq
    tiles = seq // ROW_TILE
    halo_per_tile = ROW_TILE // 16
    row = pl.BlockSpec((ROW_TILE, POOL_W), lambda b, i: (b * tiles + i, 0))
    halo = pl.BlockSpec((16, POOL_W), lambda b, i: (jnp.maximum((b * tiles + i) * halo_per_tile - 1, 0), 0))
    mem = pl.BlockSpec((mem_len, X_W), lambda b, i: (b, 0))
    return pl.pallas_call(
        _prompt_side_kernel,
        grid=(batch, tiles),
        in_specs=[row, halo, row, mem, mem],
        out_specs=[row, row],
        out_shape=[jax.ShapeDtypeStruct((n, POOL_W), F32), jax.ShapeDtypeStruct((n, X_W), F32)],
        scratch_shapes=[pltpu.VMEM((16 + ROW_TILE, POOL_W), F32)],
        compiler_params=_params(2, 32),
        name="prompt_side",
    )(u, u, qx, mk, mv)


def _sample_kernel(pt_ref, *refs, n_pages, dec_seq):
    k_pages = refs[:n_pages]
    v_pages = refs[n_pages:2 * n_pages]
    (q_ref, k_ref, v_ref, u_ref, qx_ref, mk_ref, mv_ref, buf_ref, spast_ref, sown_ref,
     pooled_ref, b_ref, c_ref, pad_ref) = refs[2 * n_pages:]
    del pt_ref
    slots = MOBA_HEADS
    rows = dec_seq * slots
    pages_per_block = MOBA_BLOCK // PAGE
    nb = n_pages // pages_per_block
    row_slot = lax.broadcasted_iota(jnp.int32, (rows, ATT_W), 0) % slots
    col = lax.broadcasted_iota(jnp.int32, (rows, ATT_W), 1)
    token_sum = (lax.broadcasted_iota(jnp.int32, (8, rows), 1) // slots
                 == lax.broadcasted_iota(jnp.int32, (8, rows), 0)).astype(BF16)

    def merge_heads(o, keep):
        return jnp.dot(token_sum, jnp.where(keep, o, 0.0).astype(BF16),
                       preferred_element_type=F32)[:dec_seq]

    keep = row_slot == col // MOBA_HEAD_DIM
    qbd = jnp.where(keep, q_ref[0] * (MOBA_HEAD_DIM ** -0.5), 0.0)
    q_hi = qbd.astype(BF16)
    q_lo = (qbd - q_hi.astype(F32)).astype(BF16)
    q_split = jnp.concatenate([q_hi, q_lo], axis=0)
    raw, gate = [], []
    for p in range(n_pages):
        s = jnp.dot(q_split, k_pages[p][...].astype(BF16), preferred_element_type=F32)
        raw.append(s[:rows])
        part = jnp.sum(s[:rows] + s[rows:], axis=-1, keepdims=True)
        if p % pages_per_block == 0:
            gate.append(part)
        else:
            gate[-1] = gate[-1] + part
    logits = []
    for p in range(n_pages):
        j = p // pages_per_block
        rank = jnp.zeros((rows, 1), jnp.int32)
        if p % pages_per_block == 0:
            for jp in range(nb):
                if jp != j:
                    ahead = (gate[jp] >= gate[j]) if jp < j else (gate[jp] > gate[j])
                    rank = rank + ahead.astype(jnp.int32)
            block_mask = jnp.where(rank < MOBA_TOPK, 0.0, MASKED)
        logits.append(raw[p] + spast_ref[:, p * PAGE:(p + 1) * PAGE] + block_mask)
    k_new = k_ref[0]
    v_new = v_ref[0]
    own = [jnp.sum(qbd * k_new[t:t + 1], axis=-1, keepdims=True) + sown_ref[:, t:t + 1]
           for t in range(dec_seq)]
    m = functools.reduce(jnp.maximum, [jnp.max(s, axis=-1, keepdims=True) for s in logits] + own)
    l = jnp.zeros((rows, 1), F32)
    o = jnp.zeros((rows, ATT_W), F32)
    for p in range(n_pages):
        pr = jnp.exp(logits[p] - m)
        l = l + jnp.sum(pr, axis=-1, keepdims=True)
        o = o + _bdot_nt(pr, v_pages[p][...])
    for t in range(dec_seq):
        pr = jnp.exp(own[t] - m)
        l = l + pr
        o = o + pr * v_new[t:t + 1]
    b_ref[0] = merge_heads(o * (1.0 / l), keep)

    keepx = row_slot == col // X_HEAD_DIM
    qxbd = jnp.where(keepx, qx_ref[0] * (X_HEAD_DIM ** -0.5), 0.0)
    s = _bdot_nt(qxbd, mk_ref[0])
    m = jnp.max(s, axis=-1, keepdims=True)
    pr = jnp.exp(s - m)
    l = jnp.sum(pr, axis=-1, keepdims=True)
    c_ref[0] = merge_heads(_bdot(pr, mv_ref[0]) * (1.0 / l), keepx)

    base = 16
    pad_ref[base - POOL_STATE:base, :] = buf_ref[0]
    pad_ref[base:base + dec_seq, :] = u_ref[0]
    pooled_ref[0] = _window_means(pad_ref, base, dec_seq, lambda w: 1.0 / w)


def _sample_mixers(page_table, cache_k, cache_v, q, k, v, u, qx, mem_k, mem_v, pool_buf, spast, sown, layer):
    dec_batch, n_pages = page_table.shape
    dec_seq = q.shape[1]
    mem_len = mem_k.shape[2]
    rows = dec_seq * MOBA_HEADS

    def page_spec(p):
        return pl.BlockSpec((None, None, ATT_W, PAGE), lambda b, pt: (layer, pt[b * n_pages + p], 0, 0))

    tok = lambda w: pl.BlockSpec((1, dec_seq, w), lambda b, pt: (b, 0, 0))
    rep = lambda w: pl.BlockSpec((1, rows, w), lambda b, pt: (b, 0, 0))
    mem = pl.BlockSpec((None, 1, mem_len, X_W), lambda b, pt: (layer, b, 0, 0))
    in_specs = ([page_spec(p) for p in range(n_pages)] * 2
                + [rep(ATT_W), tok(ATT_W), tok(ATT_W), tok(POOL_W), rep(X_W), mem, mem,
                   pl.BlockSpec((None, 1, POOL_STATE, POOL_W), lambda b, pt: (layer, b, 0, 0)),
                   pl.BlockSpec((rows, n_pages * PAGE), lambda b, pt: (0, 0)),
                   pl.BlockSpec((rows, 128), lambda b, pt: (0, 0))])
    out = lambda w: jax.ShapeDtypeStruct((dec_batch, dec_seq, w), F32)
    return pl.pallas_call(
        functools.partial(_sample_kernel, n_pages=n_pages, dec_seq=dec_seq),
        grid_spec=pltpu.PrefetchScalarGridSpec(
            num_scalar_prefetch=1,
            grid=(dec_batch,),
            in_specs=in_specs,
            out_specs=[tok(POOL_W), tok(ATT_W), tok(X_W)],
            scratch_shapes=[pltpu.VMEM((16 + 8, POOL_W), F32)]),
        out_shape=[out(POOL_W), out(ATT_W), out(X_W)],
        compiler_params=_params(1, 48),
        name="sample_mixers",
    )(page_table.reshape(-1), *([cache_k] * n_pages), *([cache_v] * n_pages),
      jnp.repeat(q, MOBA_HEADS, axis=1), k, v, u, jnp.repeat(qx, MOBA_HEADS, axis=1),
      mem_k, mem_v, pool_buf, spast, sown)


def _merge_kernel(x_ref, pooled_ref, b_ref, c_ref, gates_ref, pw_ref, ps_ref, wa_ref, wb_ref, wc_ref, wo_ref, o_ref):
    pooled = pooled_ref[...]
    mixed = jnp.concatenate(
        [_bdot(pooled[:, g * POOL_GROUP:(g + 1) * POOL_GROUP], pw_ref[g]) for g in range(len(POOL_WINDOWS))],
        axis=1) * ps_ref[...]
    merged = (gates_ref[:, :D_MODEL] * _bdot(mixed, wa_ref[...])
              + gates_ref[:, D_MODEL:2 * D_MODEL] * _bdot(b_ref[...], wb_ref[...])
              + gates_ref[:, 2 * D_MODEL:] * _bdot(c_ref[...], wc_ref[...]))
    o_ref[...] = x_ref[...] + _bdot(merged, wo_ref[...])


def _merge(x, pooled, b_out, c_out, gates, pool_w, pool_scale, w_br_a, w_br_b, w_br_c, w_out, layer):
    n = x.shape[0]
    row = lambda w: pl.BlockSpec((ROW_TILE, w), lambda i: (i, 0))
    wspec = lambda *shape: pl.BlockSpec((None,) + shape, lambda i: (layer,) + (0,) * len(shape))
    return pl.pallas_call(
        _merge_kernel,
        grid=(n // ROW_TILE,),
        in_specs=[row(D_MODEL), row(POOL_W), row(ATT_W), row(X_W), row(3 * D_MODEL),
                  wspec(len(POOL_WINDOWS), POOL_GROUP, POOL_GROUP), wspec(1, POOL_W),
                  wspec(POOL_W, D_MODEL), wspec(ATT_W, D_MODEL), wspec(X_W, D_MODEL), wspec(D_MODEL, D_MODEL)],
        out_specs=row(D_MODEL),
        out_shape=jax.ShapeDtypeStruct((n, D_MODEL), F32),
        compiler_params=_params(1, 48),
        name="merge",
    )(x, pooled, b_out, c_out, gates, pool_w, pool_scale, w_br_a, w_br_b, w_br_c, w_out)


def _ffn_kernel(x_ref, g_ref, wi_ref, wo_ref, o_ref, *, d_ff):
    x = x_ref[...]
    xb = _rmsnorm(x, g_ref[...]).astype(BF16)
    gate = jnp.dot(xb, wi_ref[:, :d_ff], preferred_element_type=F32)
    up = jnp.dot(xb, wi_ref[:, d_ff:], preferred_element_type=F32)
    act = gate * jax.nn.sigmoid(gate) * up
    o_ref[...] = x + _bdot(act, wo_ref[...])


def _ffn(x, norm2, w_ffn_in, w_ffn_out, layer):
    n = x.shape[0]
    d_ff = w_ffn_out.shape[1]
    row = pl.BlockSpec((ROW_TILE, D_MODEL), lambda i: (i, 0))
    return pl.pallas_call(
        functools.partial(_ffn_kernel, d_ff=d_ff),
        grid=(n // ROW_TILE,),
        in_specs=[row,
                  pl.BlockSpec((None, 1, D_MODEL), lambda i: (layer, 0, 0)),
                  pl.BlockSpec((None, D_MODEL, 2 * d_ff), lambda i: (layer, 0, 0)),
                  pl.BlockSpec((None, d_ff, D_MODEL), lambda i: (layer, 0, 0))],
        out_specs=row,
        out_shape=jax.ShapeDtypeStruct((n, D_MODEL), F32),
        compiler_params=_params(1, 56),
        name="ffn",
    )(x, norm2, w_ffn_in, w_ffn_out)


def _final_norm_kernel(x_ref, g_ref, o_ref):
    o_ref[...] = _rmsnorm(x_ref[...], g_ref[...])


def _final_norm(x, g):
    n = x.shape[0]
    row = pl.BlockSpec((ROW_TILE, D_MODEL), lambda i: (i, 0))
    return pl.pallas_call(
        _final_norm_kernel,
        grid=(n // ROW_TILE,),
        in_specs=[row, pl.BlockSpec((1, D_MODEL), lambda i: (0, 0))],
        out_specs=row,
        out_shape=jax.ShapeDtypeStruct((n, D_MODEL), F32),
        compiler_params=_params(1, 32),
        name="final_norm",
    )(x, g)


def kernel(x_prompt, x_sample, cache_attn_k, cache_attn_v, cache_mem_k, cache_mem_v, state_pool, page_table, mem_prompt, norm1, w_in, pool_w, pool_scale, mem_norm, w_mem_kv, w_br_a, w_br_b, w_br_c, w_out, norm2, w_ffn_in, w_ffn_out, rel_bias, final_norm):
    batch, seq, d = x_prompt.shape
    dec_batch, dec_seq = x_sample.shape[:2]
    depth = w_in.shape[0]
    n_phys = cache_attn_k.shape[1]
    n_pages = page_table.shape[1]
    mem_len = mem_prompt.shape[1]
    past_len = n_pages * PAGE
    nb = seq // MOBA_BLOCK
    n_prompt = batch * seq
    n_sample = dec_batch * dec_seq
    assert d == D_MODEL and cache_attn_k.shape[2] == PAGE
    assert seq % ROW_TILE == 0 and n_sample % ROW_TILE == 0 and nb % 8 == 0 and nb <= AUG_W - MOBA_HEAD_DIM
    assert past_len % MOBA_BLOCK == 0 and past_len // MOBA_BLOCK >= MOBA_TOPK and dec_seq <= 8
    assert (batch * mem_len) % ROW_TILE == 0 and ROW_TILE % MOBA_BLOCK == 0

    bf = lambda w: w.astype(BF16)
    w_in_b, w_mem_b, pool_w_b = bf(w_in), bf(w_mem_kv), bf(pool_w)
    w_a, w_b, w_c, w_o, w_fi, w_fo = bf(w_br_a), bf(w_br_b), bf(w_br_c), bf(w_out), bf(w_ffn_in), bf(w_ffn_out)
    norm1_r, norm2_r, mem_norm_r = (g.reshape(depth, 1, D_MODEL) for g in (norm1, norm2, mem_norm))
    pool_scale_r = pool_scale.reshape(depth, 1, POOL_W)
    cache_k = cache_attn_k.transpose(0, 1, 3, 4, 2).reshape(depth, n_phys, ATT_W, PAGE)
    cache_v = cache_attn_v.transpose(0, 1, 3, 4, 2).reshape(depth, n_phys, ATT_W, PAGE)
    mem_k_s = cache_mem_k.reshape(depth, dec_batch, mem_len, X_W)
    mem_v_s = cache_mem_v.reshape(depth, dec_batch, mem_len, X_W)
    mem_rows = mem_prompt.reshape(batch * mem_len, D_MODEL)

    bias_own, bias_prev, spast, sown = _bias_tables(rel_bias, past_len)
    spast = spast[:, :dec_seq].transpose(1, 0, 2).reshape(dec_seq * MOBA_HEADS, past_len)
    sown = sown[:, :dec_seq].transpose(1, 0, 2).reshape(dec_seq * MOBA_HEADS, 128)
    head_of_col = jnp.arange(ATT_W) // MOBA_HEAD_DIM

    x = jnp.concatenate([x_prompt.reshape(n_prompt, D_MODEL), x_sample.reshape(n_sample, D_MODEL)], axis=0)
    kp_l, vp_l, mkp_l, mvp_l, bp_l, ks_l, vs_l, bs_l = [], [], [], [], [], [], [], []
    for layer in range(depth):
        u, q, k, v, qx, gates, kmean = _in_proj(x, norm1_r, w_in_b, layer)
        mk, mv = _mem_kv(mem_rows, mem_norm_r, w_mem_b, layer)

        km = kmean[:n_prompt // MOBA_BLOCK, 0].reshape(batch, nb, 1, ATT_W)
        kmt = jnp.where(jnp.arange(MOBA_HEADS)[None, None, :, None] == head_of_col[None, None, None, :], km, 0.0)
        kmt = kmt.reshape(batch, nb * MOBA_HEADS, ATT_W)
        qa, ka, vt = _gate_augment(q, k, v, kmt, batch, seq)
        b_prompt = _moba_prompt(qa, ka, vt, bias_own, bias_prev, batch, seq)
        pooled_p, c_prompt = _prompt_side(u, qx, mk, mv, batch, seq, mem_len)

        tok = lambda a: a[n_prompt:].reshape(dec_batch, dec_seq, a.shape[-1])
        pooled_s, b_sample, c_sample = _sample_mixers(
            page_table, cache_k, cache_v, tok(q), tok(k), tok(v), tok(u), tok(qx),
            mem_k_s, mem_v_s, state_pool, spast, sown, layer)

        cat = lambda p, s: jnp.concatenate([p, s.reshape(n_sample, s.shape[-1])], axis=0)
        x = _merge(x, cat(pooled_p, pooled_s), cat(b_prompt, b_sample), cat(c_prompt, c_sample), gates,
                   pool_w_b, pool_scale_r, w_a, w_b, w_c, w_o, layer)
        x = _ffn(x, norm2_r, w_fi, w_fo, layer)

        kp_l.append(k[:n_prompt].reshape(batch, seq, MOBA_HEADS, MOBA_HEAD_DIM))
        vp_l.append(v[:n_prompt].reshape(batch, seq, MOBA_HEADS, MOBA_HEAD_DIM))
        mkp_l.append(mk.reshape(batch, mem_len, X_HEADS, X_HEAD_DIM))
        mvp_l.append(mv.reshape(batch, mem_len, X_HEADS, X_HEAD_DIM))
        bp_l.append(u[:n_prompt].reshape(batch, seq, POOL_W)[:, seq - POOL_STATE:])
        ks_l.append(tok(k).reshape(dec_batch, dec_seq, MOBA_HEADS, MOBA_HEAD_DIM))
        vs_l.append(tok(v).reshape(dec_batch, dec_seq, MOBA_HEADS, MOBA_HEAD_DIM))
        bs_l.append(jnp.concatenate([state_pool[layer], tok(u)], axis=1)[:, -POOL_STATE:])

    y = _final_norm(x, final_norm.reshape(1, D_MODEL))
    return (y[:n_prompt].reshape(batch, seq, D_MODEL), y[n_prompt:].reshape(dec_batch, dec_seq, D_MODEL),
            jnp.stack(kp_l), jnp.stack(vp_l), jnp.stack(mkp_l), jnp.stack(mvp_l),
            jnp.stack(bp_l), jnp.stack(ks_l), jnp.stack(vs_l), jnp.stack(bs_l))
```

```python
import functools
import math

import jax
import jax.numpy as jnp
from jax import lax
from jax.experimental import pallas as pl
from jax.experimental.pallas import tpu as pltpu

F32 = jnp.float32
BF16 = jnp.bfloat16

D_MODEL = 1024
POOL_W = 512
POOL_WINDOWS = (2, 4, 8, 16)
POOL_GROUP = POOL_W // len(POOL_WINDOWS)
POOL_STATE = max(POOL_WINDOWS) - 1
MOBA_HEADS = 8
MOBA_HEAD_DIM = 64
ATT_W = MOBA_HEADS * MOBA_HEAD_DIM
MOBA_BLOCK = 256
MOBA_TOPK = 3
X_HEADS = 4
X_HEAD_DIM = 128
X_W = X_HEADS * X_HEAD_DIM
N_BUCKETS = 32
MAX_EXACT = N_BUCKETS // 2
MAX_DISTANCE = 128
EPS = 1e-6
PAGE = 128

ROW_TILE = 256
AUG_W = 128
VT_ROWS = 80
LOOKAHEAD = 4
SAMPLE_SEQS = 2
MASKED = -1e30
LOG2E = math.log2(math.e)
MIB = 1 << 20


def _params(n_axes, vmem_mib):
    return pltpu.CompilerParams(dimension_semantics=("arbitrary",) * n_axes,
                                vmem_limit_bytes=vmem_mib * MIB)


def _rmsnorm(x, g):
    return x * lax.rsqrt(jnp.mean(x * x, axis=-1, keepdims=True) + EPS) * g


def _bdot(a, b):
    return jnp.dot(a.astype(BF16), b.astype(BF16), preferred_element_type=F32)


def _bdot_nt(a, b):
    return lax.dot_general(a.astype(BF16), b.astype(BF16), (((1,), (1,)), ((), ())),
                           preferred_element_type=F32)


def _bucket(rel):
    n = jnp.maximum(rel, 0)
    nf = jnp.maximum(n, 1).astype(F32)
    large = MAX_EXACT + (jnp.log(nf / MAX_EXACT) / math.log(MAX_DISTANCE / MAX_EXACT)
                         * (N_BUCKETS - MAX_EXACT)).astype(jnp.int32)
    large = jnp.minimum(large, N_BUCKETS - 1)
    return jnp.where(n < MAX_EXACT, n, large)


def _bias_lookup(rel, rb_ref, h):
    bucket = _bucket(rel)
    last = rb_ref[N_BUCKETS - 1, h]
    out = jnp.zeros(rel.shape, F32)
    for b in range(N_BUCKETS - 1):
        out = jnp.where(bucket == b, rb_ref[b, h] - last, out)
    return out


def _bias_table_kernel(rb_ref, own_ref, prev_ref, spast_ref, sown_ref, *, past_len):
    h = pl.program_id(0)
    blk = MOBA_BLOCK
    tk = lax.broadcasted_iota(jnp.int32, (blk, blk), 0)
    tq = lax.broadcasted_iota(jnp.int32, (blk, blk), 1)
    rel = tq - tk
    own_ref[0] = jnp.where(rel >= 0, _bias_lookup(rel, rb_ref, h) * LOG2E, MASKED)
    prev_ref[0] = _bias_lookup(rel + blk, rb_ref, h) * LOG2E
    t = lax.broadcasted_iota(jnp.int32, (8, past_len), 0)
    key = lax.broadcasted_iota(jnp.int32, (8, past_len), 1)
    spast_ref[0] = _bias_lookup(past_len + t - key, rb_ref, h)
    t = lax.broadcasted_iota(jnp.int32, (8, 128), 0)
    key = lax.broadcasted_iota(jnp.int32, (8, 128), 1)
    sown_ref[0] = jnp.where(key <= t, _bias_lookup(t - key, rb_ref, h), MASKED)


def _bias_tables(rel_bias, past_len):
    blk = MOBA_BLOCK
    return pl.pallas_call(
        functools.partial(_bias_table_kernel, past_len=past_len),
        grid=(MOBA_HEADS,),
        in_specs=[pl.BlockSpec(memory_space=pltpu.SMEM)],
        out_specs=[pl.BlockSpec((1, blk, blk), lambda h: (h, 0, 0)),
                   pl.BlockSpec((1, blk, blk), lambda h: (h, 0, 0)),
                   pl.BlockSpec((1, 8, past_len), lambda h: (h, 0, 0)),
                   pl.BlockSpec((1, 8, 128), lambda h: (h, 0, 0))],
        out_shape=[jax.ShapeDtypeStruct((MOBA_HEADS, blk, blk), F32),
                   jax.ShapeDtypeStruct((MOBA_HEADS, blk, blk), F32),
                   jax.ShapeDtypeStruct((MOBA_HEADS, 8, past_len), F32),
                   jax.ShapeDtypeStruct((MOBA_HEADS, 8, 128), F32)],
        compiler_params=_params(1, 32),
        name="bias_tables",
    )(rel_bias)


def _in_proj_kernel(x_ref, g_ref, w_ref, u_ref, q_ref, k_ref, v_ref, qx_ref, gates_ref, kmean_ref):
    xb = _rmsnorm(x_ref[...], g_ref[...]).astype(BF16)

    def proj(lo, n):
        return jnp.dot(xb, w_ref[:, lo:lo + n], preferred_element_type=F32)

    u_ref[...] = proj(0, POOL_W)
    q_ref[...] = proj(POOL_W, ATT_W)
    k = proj(POOL_W + ATT_W, ATT_W)
    k_ref[...] = k
    for r in range(ROW_TILE // MOBA_BLOCK):
        kmean_ref[r] = jnp.mean(k[r * MOBA_BLOCK:(r + 1) * MOBA_BLOCK], axis=0, keepdims=True)
    v_ref[...] = proj(POOL_W + 2 * ATT_W, ATT_W)
    qx_ref[...] = proj(POOL_W + 3 * ATT_W, X_W)
    gates_ref[...] = jax.nn.sigmoid(proj(POOL_W + 3 * ATT_W + X_W, 3 * D_MODEL))


def _in_proj(x, norm1, w_in, layer):
    n = x.shape[0]
    in_cols = w_in.shape[-1]
    row = lambda w: pl.BlockSpec((ROW_TILE, w), lambda i: (i, 0))
    blocks_per_tile = ROW_TILE // MOBA_BLOCK
    return pl.pallas_call(
        _in_proj_kernel,
        grid=(n // ROW_TILE,),
        in_specs=[row(D_MODEL),
                  pl.BlockSpec((None, 1, D_MODEL), lambda i: (layer, 0, 0)),
                  pl.BlockSpec((None, D_MODEL, in_cols), lambda i: (layer, 0, 0))],
        out_specs=[row(POOL_W), row(ATT_W), row(ATT_W), row(ATT_W), row(X_W), row(3 * D_MODEL),
                   pl.BlockSpec((blocks_per_tile, 1, ATT_W), lambda i: (i, 0, 0))],
        out_shape=[jax.ShapeDtypeStruct((n, POOL_W), F32), jax.ShapeDtypeStruct((n, ATT_W), F32),
                   jax.ShapeDtypeStruct((n, ATT_W), F32), jax.ShapeDtypeStruct((n, ATT_W), F32),
                   jax.ShapeDtypeStruct((n, X_W), F32), jax.ShapeDtypeStruct((n, 3 * D_MODEL), F32),
                   jax.ShapeDtypeStruct((n // MOBA_BLOCK, 1, ATT_W), F32)],
        compiler_params=_params(1, 48),
        name="in_proj",
    )(x, norm1, w_in)


def _mem_kv_kernel(x_ref, g_ref, w_ref, mk_ref, mv_ref):
    xb = _rmsnorm(x_ref[...], g_ref[...]).astype(BF16)
    mk_ref[...] = jnp.dot(xb, w_ref[:, :X_W], preferred_element_type=F32)
    mv_ref[...] = jnp.dot(xb, w_ref[:, X_W:], preferred_element_type=F32)


def _mem_kv(mem, mem_norm, w_mem_kv, layer):
    n = mem.shape[0]
    row = lambda w: pl.BlockSpec((ROW_TILE, w), lambda i: (i, 0))
    return pl.pallas_call(
        _mem_kv_kernel,
        grid=(n // ROW_TILE,),
        in_specs=[row(D_MODEL),
                  pl.BlockSpec((None, 1, D_MODEL), lambda i: (layer, 0, 0)),
                  pl.BlockSpec((None, D_MODEL, 2 * X_W), lambda i: (layer, 0, 0))],
        out_specs=[row(X_W), row(X_W)],
        out_shape=[jax.ShapeDtypeStruct((n, X_W), F32), jax.ShapeDtypeStruct((n, X_W), F32)],
        compiler_params=_params(1, 32),
        name="mem_kv",
    )(mem, mem_norm, w_mem_kv)


def _gate_augment_kernel(q_ref, k_ref, v_ref, kmt_ref, kt_prev_ref, vt_prev_ref,
                         qa_ref, ka_ref, vt_ref, kt_all_ref, vt_all_ref, sel_ref, *, nb):
    del kt_prev_ref, vt_prev_ref
    i = pl.program_id(1)
    blk = MOBA_BLOCK
    qt = q_ref[...].T
    st = jnp.dot(kmt_ref[0], qt, precision=lax.Precision.HIGHEST,
                 preferred_element_type=F32)
    row_block = lax.broadcasted_iota(jnp.int32, st.shape, 0) // MOBA_HEADS
    st = jnp.where(row_block < i, st, -jnp.inf)
    slabs = [st[j * MOBA_HEADS:(j + 1) * MOBA_HEADS] for j in range(nb)]
    for j in range(nb):
        rank = jnp.zeros((MOBA_HEADS, blk), jnp.int32)
        for jp in range(nb):
            if jp == j:
                continue
            ahead = (slabs[jp] >= slabs[j]) if jp < j else (slabs[jp] > slabs[j])
            rank = rank + ahead.astype(jnp.int32)
        limit = jnp.where(j < i, MOBA_TOPK, jnp.where(j == i, nb + 1, 0))
        mask = jnp.where(rank < limit, 0.0, MASKED)
        for c in range(blk // 128):
            sel_ref[c, j * MOBA_HEADS:(j + 1) * MOBA_HEADS, :] = mask[:, c * 128:(c + 1) * 128]

    pad_rows = AUG_W - MOBA_HEAD_DIM - nb
    for h in range(MOBA_HEADS):
        pieces = [qt[h * MOBA_HEAD_DIM:(h + 1) * MOBA_HEAD_DIM] * (LOG2E * MOBA_HEAD_DIM ** -0.5),
                  jnp.concatenate([sel_ref[c, pl.ds(h, nb, stride=MOBA_HEADS), :]
                                   for c in range(blk // 128)], axis=1)]
        if pad_rows:
            pieces.append(jnp.zeros((pad_rows, blk), F32))
        qa_ref[0, h] = jnp.concatenate(pieces, axis=0).astype(BF16)

    lane = lax.broadcasted_iota(jnp.int32, (blk, AUG_W), 1)
    block_col = jnp.where(lane == MOBA_HEAD_DIM + i, 1.0, 0.0)
    k = k_ref[...]
    for pair in range(MOBA_HEADS // 2):
        x = k[:, pair * AUG_W:(pair + 1) * AUG_W]
        ka_ref[0, 2 * pair, 0] = jnp.where(lane < MOBA_HEAD_DIM, x, block_col).astype(BF16)
        ka_ref[0, 2 * pair + 1, 0] = jnp.where(lane < MOBA_HEAD_DIM, pltpu.roll(x, MOBA_HEAD_DIM, 1),
                                               block_col).astype(BF16)
    kt_all_ref[...] = k.T
    vt = v_ref[...].T
    vt_all_ref[...] = vt
    ones_row = jnp.where(lax.broadcasted_iota(jnp.int32, (VT_ROWS - MOBA_HEAD_DIM, blk), 0) == 0, 1.0, 0.0)
    for h in range(MOBA_HEADS):
        vt_ref[0, h, 0] = jnp.concatenate([vt[h * MOBA_HEAD_DIM:(h + 1) * MOBA_HEAD_DIM], ones_row],
                                          axis=0).astype(BF16)


def _gate_augment(q, k, v, kmt, kt_all, vt_all, layer, batch, seq):
    nb = seq // MOBA_BLOCK
    blk = MOBA_BLOCK
    row = pl.BlockSpec((blk, ATT_W), lambda b, i: (b * nb + i, 0))
    slab = pl.BlockSpec((None, None, ATT_W, blk), lambda b, i: (layer, b, 0, i))
    return pl.pallas_call(
        functools.partial(_gate_augment_kernel, nb=nb),
        grid=(batch, nb),
        in_specs=[row, row, row, pl.BlockSpec((1, nb * MOBA_HEADS, ATT_W), lambda b, i: (b, 0, 0)),
                  pl.BlockSpec(memory_space=pl.ANY), pl.BlockSpec(memory_space=pl.ANY)],
        out_specs=[pl.BlockSpec((1, MOBA_HEADS, AUG_W, blk), lambda b, i: (b, 0, 0, i)),
                   pl.BlockSpec((1, MOBA_HEADS, 1, blk, AUG_W), lambda b, i: (b, 0, i, 0, 0)),
                   pl.BlockSpec((1, MOBA_HEADS, 1, VT_ROWS, blk), lambda b, i: (b, 0, i, 0, 0)),
                   slab, slab],
        out_shape=[jax.ShapeDtypeStruct((batch, MOBA_HEADS, AUG_W, seq), BF16),
                   jax.ShapeDtypeStruct((batch, MOBA_HEADS, nb, blk, AUG_W), BF16),
                   jax.ShapeDtypeStruct((batch, MOBA_HEADS, nb, VT_ROWS, blk), BF16),
                   jax.ShapeDtypeStruct(kt_all.shape, F32), jax.ShapeDtypeStruct(vt_all.shape, F32)],
        input_output_aliases={4: 3, 5: 4},
        scratch_shapes=[pltpu.VMEM((blk // 128, nb * MOBA_HEADS, 128), F32)],
        compiler_params=_params(2, 32),
        name="gate_augment",
    )(q, k, v, kmt, kt_all, vt_all)


def _moba_kernel(qa_ref, ka_ref, vt_ref, own_ref, prev_ref, o_ref, m_ref, acc_ref):
    i = pl.program_id(1)

    def scores(h, j, bias):
        s = jnp.dot(ka_ref[0, h, j], qa_ref[0, h], preferred_element_type=F32)
        return s if bias is None else s + bias[h]

    def fold(h, j, s, first):
        m_blk = jnp.max(s, axis=0, keepdims=True)
        m_new = m_blk if first else jnp.maximum(m_ref[h], m_blk)
        p = jnp.exp2(s - m_new)
        pv = jnp.dot(vt_ref[0, h, j], p.astype(BF16), preferred_element_type=F32)
        if first:
            acc_ref[h] = pv
        else:
            acc_ref[h] = jnp.exp2(m_ref[h] - m_new) * acc_ref[h] + pv
        m_ref[h] = m_new

    def block(j, bias, first):
        pending = [scores(h, j, bias) for h in range(LOOKAHEAD)]
        for h in range(MOBA_HEADS):
            s = pending.pop(0)
            if h + LOOKAHEAD < MOBA_HEADS:
                pending.append(scores(h + LOOKAHEAD, j, bias))
            fold(h, j, s, first)

    block(i, own_ref, True)

    def far_blocks(j, carry):
        block(j, None, False)
        return carry

    lax.fori_loop(0, i - 1, far_blocks, 0)

    @pl.when(i >= 1)
    def _():
        block(i - 1, prev_ref, False)

    for pair in range(MOBA_HEADS // 2):
        outs = [acc_ref[h, :MOBA_HEAD_DIM] * (1.0 / acc_ref[h, MOBA_HEAD_DIM:MOBA_HEAD_DIM + 1])
                for h in (2 * pair, 2 * pair + 1)]
        o_ref[:, pair * AUG_W:(pair + 1) * AUG_W] = jnp.concatenate(outs, axis=0).T


def _moba_prompt(qa, ka, vt, bias_own, bias_prev, batch, seq):
    nb = seq // MOBA_BLOCK
    blk = MOBA_BLOCK
    return pl.pallas_call(
        _moba_kernel,
        grid=(batch, nb),
        in_specs=[pl.BlockSpec((1, MOBA_HEADS, AUG_W, blk), lambda b, i: (b, 0, 0, i)),
                  pl.BlockSpec((1, MOBA_HEADS, nb, blk, AUG_W), lambda b, i: (b, 0, 0, 0, 0)),
                  pl.BlockSpec((1, MOBA_HEADS, nb, VT_ROWS, blk), lambda b, i: (b, 0, 0, 0, 0)),
                  pl.BlockSpec((MOBA_HEADS, blk, blk), lambda b, i: (0, 0, 0)),
                  pl.BlockSpec((MOBA_HEADS, blk, blk), lambda b, i: (0, 0, 0))],
        out_specs=pl.BlockSpec((blk, ATT_W), lambda b, i: (b * nb + i, 0)),
        out_shape=jax.ShapeDtypeStruct((batch * seq, ATT_W), F32),
        scratch_shapes=[pltpu.VMEM((MOBA_HEADS, 1, blk), F32), pltpu.VMEM((MOBA_HEADS, VT_ROWS, blk), F32)],
        compiler_params=_params(2, 48),
        name="moba_prompt",
    )(qa, ka, vt, bias_own, bias_prev)


def _window_means(pad_ref, base, rows, inv_count):
    groups = []
    for g, w in enumerate(POOL_WINDOWS):
        cols = slice(g * POOL_GROUP, (g + 1) * POOL_GROUP)
        tok = pad_ref[base:base + rows, cols]
        acc = tok
        for s in range(1, w):
            acc = acc + pad_ref[base - s:base - s + rows, cols]
        groups.append(acc * inv_count(w) - tok)
    return jnp.concatenate(groups, axis=1)


def _cross_attend(qx, mk, mv):
    outs = []
    for h in range(X_HEADS):
        cols = slice(h * X_HEAD_DIM, (h + 1) * X_HEAD_DIM)
        s = _bdot_nt(qx[:, cols], mk[:, cols]) * (X_HEAD_DIM ** -0.5)
        m = jnp.max(s, axis=-1, keepdims=True)
        p = jnp.exp(s - m)
        l = jnp.sum(p, axis=-1, keepdims=True)
        outs.append(_bdot(p, mv[:, cols]) * (1.0 / l))
    return jnp.concatenate(outs, axis=1)


def _prompt_side_kernel(u_ref, halo_ref, qx_ref, mk_ref, mv_ref, pooled_ref, c_ref, pad_ref):
    i = pl.program_id(1)
    rows = ROW_TILE
    base = 16
    first = jnp.full((base, 1), i, jnp.int32) == 0
    pad_ref[0:base, :] = jnp.where(first, 0.0, halo_ref[...])
    pad_ref[base:base + rows, :] = u_ref[...]
    pos = i * rows + lax.broadcasted_iota(jnp.int32, (rows, 1), 0)

    def inv_count(w):
        return 1.0 / jnp.minimum(pos + 1, w).astype(F32)

    pooled_ref[...] = _window_means(pad_ref, base, rows, inv_count)
    c_ref[...] = _cross_attend(qx_ref[...], mk_ref[...].astype(BF16), mv_ref[...].astype(BF16))


def _prompt_side(u, qx, mk, mv, batch, seq, mem_len):
    n = batch * seq
    tiles = seq // ROW_TILE
    halo_per_tile = ROW_TILE // 16
    row = pl.BlockSpec((ROW_TILE, POOL_W), lambda b, i: (b * tiles + i, 0))
    halo = pl.BlockSpec((16, POOL_W), lambda b, i: (jnp.maximum((b * tiles + i) * halo_per_tile - 1, 0), 0))
    mem = pl.BlockSpec((mem_len, X_W), lambda b, i: (b, 0))
    return pl.pallas_call(
        _prompt_side_kernel,
        grid=(batch, tiles),
        in_specs=[row, halo, row, mem, mem],
        out_specs=[row, row],
        out_shape=[jax.ShapeDtypeStruct((n, POOL_W), F32), jax.ShapeDtypeStruct((n, X_W), F32)],
        scratch_shapes=[pltpu.VMEM((16 + ROW_TILE, POOL_W), F32)],
        compiler_params=_params(2, 32),
        name="prompt_side",
    )(u, u, qx, mk, mv)


def _sample_kernel(pt_ref, *refs, n_pages, dec_seq):
    seqs = SAMPLE_SEQS
    k_pages = refs[:seqs * n_pages]
    v_pages = refs[seqs * n_pages:2 * seqs * n_pages]
    (q_ref, k_ref, v_ref, u_ref, qx_ref, mk_ref, mv_ref, buf_ref, spast_ref, sown_ref,
     pooled_ref, b_ref, c_ref, pad_ref) = refs[2 * seqs * n_pages:]
    del pt_ref
    toks = seqs * dec_seq
    rows = MOBA_HEADS * toks
    pages_per_block = MOBA_BLOCK // PAGE
    nb = n_pages // pages_per_block
    tok_seq = lax.broadcasted_iota(jnp.int32, (toks, 1), 0) // dec_seq
    col_head = lax.broadcasted_iota(jnp.int32, (toks, ATT_W), 1) // MOBA_HEAD_DIM

    q_tok = q_ref[...] * (MOBA_HEAD_DIM ** -0.5)
    qbd = jnp.concatenate([jnp.where(col_head == h, q_tok, 0.0) for h in range(MOBA_HEADS)], axis=0)
    q_hi = qbd.astype(BF16)
    q_lo = (qbd - q_hi.astype(F32)).astype(BF16)
    q_split = jnp.concatenate([q_hi, q_lo], axis=0)
    k_new = k_ref[...]
    v_new = v_ref[...]
    b_out = jnp.zeros((toks, ATT_W), F32)
    for a in range(seqs):
        b_seq = _sample_moba(q_split, qbd, k_pages[a * n_pages:(a + 1) * n_pages],
                             v_pages[a * n_pages:(a + 1) * n_pages], k_new[a * dec_seq:(a + 1) * dec_seq],
                             v_new[a * dec_seq:(a + 1) * dec_seq], spast_ref, sown_ref, nb)
        merged = functools.reduce(
            jnp.add, [jnp.where(col_head == h, b_seq[h * toks:(h + 1) * toks], 0.0) for h in range(MOBA_HEADS)])
        b_out = jnp.where(tok_seq == a, merged, b_out)
    b_ref[...] = b_out

    qx_tok = qx_ref[...] * (X_HEAD_DIM ** -0.5)
    qx_rows = jnp.concatenate([qx_tok[:, h * X_HEAD_DIM:(h + 1) * X_HEAD_DIM] for h in range(X_HEADS)],
                              axis=0)
    mem_rows = mk_ref.shape[1]
    same_head = (lax.broadcasted_iota(jnp.int32, (X_HEADS * toks, mem_rows), 1) % X_HEADS
                 == lax.broadcasted_iota(jnp.int32, (X_HEADS * toks, mem_rows), 0) // toks)
    c_out = jnp.zeros((toks, X_W), F32)
    for a in range(seqs):
        s = jnp.where(same_head, _bdot_nt(qx_rows, mk_ref[a]), MASKED)
        m = jnp.max(s, axis=-1, keepdims=True)
        pr = jnp.exp(s - m)
        l = jnp.sum(pr, axis=-1, keepdims=True)
        o = _bdot(pr, mv_ref[a]) * (1.0 / l)
        merged = jnp.concatenate([o[h * toks:(h + 1) * toks] for h in range(X_HEADS)], axis=1)
        c_out = jnp.where(tok_seq == a, merged, c_out)
    c_ref[...] = c_out

    base = 16
    u_tok = u_ref[...]
    for a in range(seqs):
        pad = pad_ref.at[a]
        pad[base - POOL_STATE:base, :] = buf_ref[a]
        pad[base:base + dec_seq, :] = u_tok[a * dec_seq:(a + 1) * dec_seq]
        pooled_ref[a * dec_seq:(a + 1) * dec_seq, :] = _window_means(pad, base, dec_seq, lambda w: 1.0 / w)


def _sample_moba(q_split, qbd, k_pages, v_pages, k_new, v_new, spast_ref, sown_ref, nb):
    rows = qbd.shape[0]
    n_pages = len(k_pages)
    dec_seq = k_new.shape[0]
    pages_per_block = n_pages // nb
    raw, gate = [], []
    for p in range(n_pages):
        s = jnp.dot(q_split, k_pages[p][...].astype(BF16), preferred_element_type=F32)
        raw.append(s[:rows])
        part = jnp.sum(s[:rows] + s[rows:], axis=-1, keepdims=True)
        if p % pages_per_block == 0:
            gate.append(part)
        else:
            gate[-1] = gate[-1] + part
    logits = []
    for p in range(n_pages):
        j = p // pages_per_block
        rank = jnp.zeros((rows, 1), jnp.int32)
        if p % pages_per_block == 0:
            for jp in range(nb):
                if jp != j:
                    ahead = (gate[jp] >= gate[j]) if jp < j else (gate[jp] > gate[j])
                    rank = rank + ahead.astype(jnp.int32)
            block_mask = jnp.where(rank < MOBA_TOPK, 0.0, MASKED)
        logits.append(raw[p] + spast_ref[:, p * PAGE:(p + 1) * PAGE] + block_mask)
    own = [jnp.sum(qbd * k_new[t:t + 1], axis=-1, keepdims=True) + sown_ref[:, t:t + 1]
           for t in range(dec_seq)]
    m = functools.reduce(jnp.maximum, [jnp.max(s, axis=-1, keepdims=True) for s in logits] + own)
    l = jnp.zeros((rows, 1), F32)
    o = jnp.zeros((rows, ATT_W), F32)
    for p in range(n_pages):
        pr = jnp.exp(logits[p] - m)
        l = l + jnp.sum(pr, axis=-1, keepdims=True)
        o = o + _bdot_nt(pr, v_pages[p][...])
    for t in range(dec_seq):
        pr = jnp.exp(own[t] - m)
        l = l + pr
        o = o + pr * v_new[t:t + 1]
    return o * (1.0 / l)


def _sample_mixers(page_table, cache_k, cache_v, q, k, v, u, qx, mem_k, mem_v, pool_buf, spast, sown, layer,
                   n_prompt):
    dec_batch, n_pages = page_table.shape
    seqs = SAMPLE_SEQS
    dec_seq = (q.shape[0] - n_prompt) // dec_batch
    toks = seqs * dec_seq
    rows = toks * MOBA_HEADS
    mem_rows = mem_k.shape[2]

    def page_spec(a, p):
        return pl.BlockSpec((None, None, ATT_W, PAGE),
                            lambda b, pt: (layer, pt[(b * seqs + a) * n_pages + p], 0, 0))

    pages = [page_spec(a, p) for a in range(seqs) for p in range(n_pages)]
    tok = lambda w: pl.BlockSpec((toks, w), lambda b, pt: (n_prompt // toks + b, 0))
    mem = pl.BlockSpec((None, seqs, mem_rows, X_HEAD_DIM), lambda b, pt: (layer, b, 0, 0))
    in_specs = (pages * 2
                + [tok(ATT_W), tok(ATT_W), tok(ATT_W), tok(POOL_W), tok(X_W), mem, mem,
                   pl.BlockSpec((None, seqs, POOL_STATE, POOL_W), lambda b, pt: (layer, b, 0, 0)),
                   pl.BlockSpec((rows, n_pages * PAGE), lambda b, pt: (0, 0)),
                   pl.BlockSpec((rows, 128), lambda b, pt: (0, 0))])
    out_spec = lambda w: pl.BlockSpec((toks, w), lambda b, pt: (b, 0))
    out = lambda w: jax.ShapeDtypeStruct((dec_batch * dec_seq, w), F32)
    return pl.pallas_call(
        functools.partial(_sample_kernel, n_pages=n_pages, dec_seq=dec_seq),
        grid_spec=pltpu.PrefetchScalarGridSpec(
            num_scalar_prefetch=1,
            grid=(dec_batch // seqs,),
            in_specs=in_specs,
            out_specs=[out_spec(POOL_W), out_spec(ATT_W), out_spec(X_W)],
            scratch_shapes=[pltpu.VMEM((seqs, 16 + 8, POOL_W), F32)]),
        out_shape=[out(POOL_W), out(ATT_W), out(X_W)],
        compiler_params=_params(1, 52),
        name="sample_mixers",
    )(page_table.reshape(-1), *([cache_k] * (seqs * n_pages)), *([cache_v] * (seqs * n_pages)),
      q, k, v, u, qx, mem_k, mem_v, pool_buf, spast, sown)


def _merge_kernel(x_ref, pooled_p_ref, pooled_s_ref, b_p_ref, b_s_ref, c_p_ref, c_s_ref, gates_ref,
                  pw_ref, ps_ref, wa_ref, wb_ref, wc_ref, wo_ref, o_ref, *, prompt_tiles):
    is_sample = jnp.full((ROW_TILE, 1), pl.program_id(0), jnp.int32) >= prompt_tiles
    pooled = jnp.where(is_sample, pooled_s_ref[...], pooled_p_ref[...])
    b_out = jnp.where(is_sample, b_s_ref[...], b_p_ref[...])
    c_out = jnp.where(is_sample, c_s_ref[...], c_p_ref[...])
    mixed = jnp.concatenate(
        [_bdot(pooled[:, g * POOL_GROUP:(g + 1) * POOL_GROUP], pw_ref[g]) for g in range(len(POOL_WINDOWS))],
        axis=1) * ps_ref[...]
    merged = (gates_ref[:, :D_MODEL] * _bdot(mixed, wa_ref[...])
              + gates_ref[:, D_MODEL:2 * D_MODEL] * _bdot(b_out, wb_ref[...])
              + gates_ref[:, 2 * D_MODEL:] * _bdot(c_out, wc_ref[...]))
    o_ref[...] = x_ref[...] + _bdot(merged, wo_ref[...])


def _merge(x, pooled, b_out, c_out, gates, pool_w, pool_scale, w_br_a, w_br_b, w_br_c, w_out, layer):
    n = x.shape[0]
    prompt_tiles = pooled[0].shape[0] // ROW_TILE
    row = lambda w: pl.BlockSpec((ROW_TILE, w), lambda i: (i, 0))
    prompt = lambda w: pl.BlockSpec((ROW_TILE, w), lambda i: (jnp.minimum(i, prompt_tiles - 1), 0))
    sample = lambda w: pl.BlockSpec((ROW_TILE, w), lambda i: (jnp.maximum(i - prompt_tiles, 0), 0))
    wspec = lambda *shape: pl.BlockSpec((None,) + shape, lambda i: (layer,) + (0,) * len(shape))
    return pl.pallas_call(
        functools.partial(_merge_kernel, prompt_tiles=prompt_tiles),
        grid=(n // ROW_TILE,),
        in_specs=[row(D_MODEL), prompt(POOL_W), sample(POOL_W), prompt(ATT_W), sample(ATT_W),
                  prompt(X_W), sample(X_W), row(3 * D_MODEL),
                  wspec(len(POOL_WINDOWS), POOL_GROUP, POOL_GROUP), wspec(1, POOL_W),
                  wspec(POOL_W, D_MODEL), wspec(ATT_W, D_MODEL), wspec(X_W, D_MODEL), wspec(D_MODEL, D_MODEL)],
        out_specs=row(D_MODEL),
        out_shape=jax.ShapeDtypeStruct((n, D_MODEL), F32),
        compiler_params=_params(1, 48),
        name="merge",
    )(x, pooled[0], pooled[1], b_out[0], b_out[1], c_out[0], c_out[1], gates,
      pool_w, pool_scale, w_br_a, w_br_b, w_br_c, w_out)


def _ffn_kernel(x_ref, g_ref, wi_ref, wo_ref, o_ref, *, d_ff):
    x = x_ref[...]
    xb = _rmsnorm(x, g_ref[...]).astype(BF16)
    gate = jnp.dot(xb, wi_ref[:, :d_ff], preferred_element_type=F32)
    up = jnp.dot(xb, wi_ref[:, d_ff:], preferred_element_type=F32)
    act = gate * jax.nn.sigmoid(gate) * up
    o_ref[...] = x + _bdot(act, wo_ref[...])


def _ffn(x, norm2, w_ffn_in, w_ffn_out, layer):
    n = x.shape[0]
    d_ff = w_ffn_out.shape[1]
    row = pl.BlockSpec((ROW_TILE, D_MODEL), lambda i: (i, 0))
    return pl.pallas_call(
        functools.partial(_ffn_kernel, d_ff=d_ff),
        grid=(n // ROW_TILE,),
        in_specs=[row,
                  pl.BlockSpec((None, 1, D_MODEL), lambda i: (layer, 0, 0)),
                  pl.BlockSpec((None, D_MODEL, 2 * d_ff), lambda i: (layer, 0, 0)),
                  pl.BlockSpec((None, d_ff, D_MODEL), lambda i: (layer, 0, 0))],
        out_specs=row,
        out_shape=jax.ShapeDtypeStruct((n, D_MODEL), F32),
        compiler_params=_params(1, 56),
        name="ffn",
    )(x, norm2, w_ffn_in, w_ffn_out)


def _final_norm_kernel(x_ref, g_ref, o_ref):
    o_ref[...] = _rmsnorm(x_ref[...], g_ref[...])


def _final_norm(x, g):
    n = x.shape[0]
    row = pl.BlockSpec((ROW_TILE, D_MODEL), lambda i: (i, 0))
    return pl.pallas_call(
        _final_norm_kernel,
        grid=(n // ROW_TILE,),
        in_specs=[row, pl.BlockSpec((1, D_MODEL), lambda i: (0, 0))],
        out_specs=row,
        out_shape=jax.ShapeDtypeStruct((n, D_MODEL), F32),
        compiler_params=_params(1, 32),
        name="final_norm",
    )(x, g)


def kernel(x_prompt, x_sample, cache_attn_k, cache_attn_v, cache_mem_k, cache_mem_v, state_pool, page_table, mem_prompt, norm1, w_in, pool_w, pool_scale, mem_norm, w_mem_kv, w_br_a, w_br_b, w_br_c, w_out, norm2, w_ffn_in, w_ffn_out, rel_bias, final_norm):
    batch, seq, d = x_prompt.shape
    dec_batch, dec_seq = x_sample.shape[:2]
    depth = w_in.shape[0]
    n_phys = cache_attn_k.shape[1]
    n_pages = page_table.shape[1]
    mem_len = mem_prompt.shape[1]
    past_len = n_pages * PAGE
    nb = seq // MOBA_BLOCK
    n_prompt = batch * seq
    n_sample = dec_batch * dec_seq
    assert d == D_MODEL and cache_attn_k.shape[2] == PAGE
    assert seq % ROW_TILE == 0 and n_sample % ROW_TILE == 0 and nb % 8 == 0 and nb <= AUG_W - MOBA_HEAD_DIM
    assert past_len % MOBA_BLOCK == 0 and past_len // MOBA_BLOCK >= MOBA_TOPK
    assert SAMPLE_SEQS * dec_seq == 8 and dec_batch % SAMPLE_SEQS == 0
    assert (batch * mem_len) % ROW_TILE == 0 and ROW_TILE % MOBA_BLOCK == 0

    bf = lambda w: w.astype(BF16)
    w_in_b, w_mem_b, pool_w_b = bf(w_in), bf(w_mem_kv), bf(pool_w)
    w_a, w_b, w_c, w_o, w_fi, w_fo = bf(w_br_a), bf(w_br_b), bf(w_br_c), bf(w_out), bf(w_ffn_in), bf(w_ffn_out)
    norm1_r, norm2_r, mem_norm_r = (g.reshape(depth, 1, D_MODEL) for g in (norm1, norm2, mem_norm))
    pool_scale_r = pool_scale.reshape(depth, 1, POOL_W)
    cache_k = cache_attn_k.transpose(0, 1, 3, 4, 2).reshape(depth, n_phys, ATT_W, PAGE)
    cache_v = cache_attn_v.transpose(0, 1, 3, 4, 2).reshape(depth, n_phys, ATT_W, PAGE)
    mem_k_s = cache_mem_k.reshape(depth, dec_batch, mem_len * X_HEADS, X_HEAD_DIM)
    mem_v_s = cache_mem_v.reshape(depth, dec_batch, mem_len * X_HEADS, X_HEAD_DIM)
    mem_rows = mem_prompt.reshape(batch * mem_len, D_MODEL)

    bias_own, bias_prev, spast, sown = _bias_tables(rel_bias, past_len)

    def sample_table(t):
        t = jnp.broadcast_to(t[:, None, :dec_seq], (MOBA_HEADS, SAMPLE_SEQS, dec_seq, t.shape[-1]))
        return t.reshape(MOBA_HEADS * SAMPLE_SEQS * dec_seq, t.shape[-1])

    spast, sown = sample_table(spast), sample_table(sown)
    head_of_col = jnp.arange(ATT_W) // MOBA_HEAD_DIM

    x = jnp.concatenate([x_prompt.reshape(n_prompt, D_MODEL), x_sample.reshape(n_sample, D_MODEL)], axis=0)
    kt_all = jnp.zeros((depth, batch, ATT_W, seq), F32)
    vt_all = jnp.zeros((depth, batch, ATT_W, seq), F32)
    mkp_l, mvp_l, bp_l, ks_l, vs_l, bs_l = [], [], [], [], [], []
    for layer in range(depth):
        u, q, k, v, qx, gates, kmean = _in_proj(x, norm1_r, w_in_b, layer)
        mk, mv = _mem_kv(mem_rows, mem_norm_r, w_mem_b, layer)

        km = kmean[:n_prompt // MOBA_BLOCK, 0].reshape(batch, nb, 1, ATT_W)
        kmt = jnp.where(jnp.arange(MOBA_HEADS)[None, None, :, None] == head_of_col[None, None, None, :], km, 0.0)
        kmt = kmt.reshape(batch, nb * MOBA_HEADS, ATT_W)
        qa, ka, vt, kt_all, vt_all = _gate_augment(q, k, v, kmt, kt_all, vt_all, layer, batch, seq)
        b_prompt = _moba_prompt(qa, ka, vt, bias_own, bias_prev, batch, seq)
        pooled_p, c_prompt = _prompt_side(u, qx, mk, mv, batch, seq, mem_len)

        pooled_s, b_sample, c_sample = _sample_mixers(
            page_table, cache_k, cache_v, q, k, v, u, qx, mem_k_s, mem_v_s, state_pool, spast, sown, layer, n_prompt)

        x = _merge(x, (pooled_p, pooled_s), (b_prompt, b_sample), (c_prompt, c_sample), gates,
                   pool_w_b, pool_scale_r, w_a, w_b, w_c, w_o, layer)
        x = _ffn(x, norm2_r, w_fi, w_fo, layer)

        tok = lambda a: a[n_prompt:].reshape(dec_batch, dec_seq, a.shape[-1])
        mkp_l.append(mk.reshape(batch, mem_len, X_HEADS, X_HEAD_DIM))
        mvp_l.append(mv.reshape(batch, mem_len, X_HEADS, X_HEAD_DIM))
        bp_l.append(jnp.stack([u[(b + 1) * seq - POOL_STATE:(b + 1) * seq] for b in range(batch)]))
        ks_l.append(tok(k).reshape(dec_batch, dec_seq, MOBA_HEADS, MOBA_HEAD_DIM))
        vs_l.append(tok(v).reshape(dec_batch, dec_seq, MOBA_HEADS, MOBA_HEAD_DIM))
        bs_l.append(jnp.concatenate([state_pool[layer], tok(u)], axis=1)[:, -POOL_STATE:])

    y = _final_norm(x, final_norm.reshape(1, D_MODEL))
    heads_last = lambda t: t.reshape(depth, batch, MOBA_HEADS, MOBA_HEAD_DIM, seq).transpose(0, 1, 4, 2, 3)
    return (y[:n_prompt].reshape(batch, seq, D_MODEL), y[n_prompt:].reshape(dec_batch, dec_seq, D_MODEL),
            heads_last(kt_all), heads_last(vt_all), jnp.stack(mkp_l), jnp.stack(mvp_l),
            jnp.stack(bp_l), jnp.stack(ks_l), jnp.stack(vs_l), jnp.stack(bs_l))
```

```python
import functools
import math

import jax
import jax.numpy as jnp
from jax import lax
from jax.experimental import pallas as pl
from jax.experimental.pallas import tpu as pltpu

F32 = jnp.float32
BF16 = jnp.bfloat16

D_MODEL = 1024
POOL_W = 512
POOL_WINDOWS = (2, 4, 8, 16)
POOL_GROUP = POOL_W // len(POOL_WINDOWS)
POOL_STATE = max(POOL_WINDOWS) - 1
MOBA_HEADS = 8
MOBA_HEAD_DIM = 64
ATT_W = MOBA_HEADS * MOBA_HEAD_DIM
MOBA_BLOCK = 256
MOBA_TOPK = 3
X_HEADS = 4
X_HEAD_DIM = 128
X_W = X_HEADS * X_HEAD_DIM
N_BUCKETS = 32
MAX_EXACT = N_BUCKETS // 2
MAX_DISTANCE = 128
EPS = 1e-6
PAGE = 128

ROW_TILE = 256
AUG_W = 128
VT_ROWS = 80
LOOKAHEAD = 4
SAMPLE_SEQS = 2
MASKED = -1e30
LOG2E = math.log2(math.e)
MIB = 1 << 20


def _params(n_axes, vmem_mib):
    return pltpu.CompilerParams(dimension_semantics=("arbitrary",) * n_axes,
                                vmem_limit_bytes=vmem_mib * MIB)


def _rmsnorm(x, g):
    return x * lax.rsqrt(jnp.mean(x * x, axis=-1, keepdims=True) + EPS) * g


def _bdot(a, b):
    return jnp.dot(a.astype(BF16), b.astype(BF16), preferred_element_type=F32)


def _bdot_nt(a, b):
    return lax.dot_general(a.astype(BF16), b.astype(BF16), (((1,), (1,)), ((), ())),
                           preferred_element_type=F32)


def _bucket(rel):
    n = jnp.maximum(rel, 0)
    nf = jnp.maximum(n, 1).astype(F32)
    large = MAX_EXACT + (jnp.log(nf / MAX_EXACT) / math.log(MAX_DISTANCE / MAX_EXACT)
                         * (N_BUCKETS - MAX_EXACT)).astype(jnp.int32)
    large = jnp.minimum(large, N_BUCKETS - 1)
    return jnp.where(n < MAX_EXACT, n, large)


def _bias_lookup(rel, rb_ref, h):
    bucket = _bucket(rel)
    last = rb_ref[N_BUCKETS - 1, h]
    out = jnp.zeros(rel.shape, F32)
    for b in range(N_BUCKETS - 1):
        out = jnp.where(bucket == b, rb_ref[b, h] - last, out)
    return out


def _bias_table_kernel(rb_ref, own_ref, prev_ref, spast_ref, sown_ref, *, past_len):
    h = pl.program_id(0)
    blk = MOBA_BLOCK
    tk = lax.broadcasted_iota(jnp.int32, (blk, blk), 0)
    tq = lax.broadcasted_iota(jnp.int32, (blk, blk), 1)
    rel = tq - tk
    own_ref[0] = jnp.where(rel >= 0, _bias_lookup(rel, rb_ref, h) * LOG2E, MASKED)
    prev_ref[0] = _bias_lookup(rel + blk, rb_ref, h) * LOG2E
    t = lax.broadcasted_iota(jnp.int32, (8, past_len), 0)
    key = lax.broadcasted_iota(jnp.int32, (8, past_len), 1)
    spast_ref[0] = _bias_lookup(past_len + t - key, rb_ref, h)
    t = lax.broadcasted_iota(jnp.int32, (8, 128), 0)
    key = lax.broadcasted_iota(jnp.int32, (8, 128), 1)
    sown_ref[0] = jnp.where(key <= t, _bias_lookup(t - key, rb_ref, h), MASKED)


def _bias_tables(rel_bias, past_len):
    blk = MOBA_BLOCK
    return pl.pallas_call(
        functools.partial(_bias_table_kernel, past_len=past_len),
        grid=(MOBA_HEADS,),
        in_specs=[pl.BlockSpec(memory_space=pltpu.SMEM)],
        out_specs=[pl.BlockSpec((1, blk, blk), lambda h: (h, 0, 0)),
                   pl.BlockSpec((1, blk, blk), lambda h: (h, 0, 0)),
                   pl.BlockSpec((1, 8, past_len), lambda h: (h, 0, 0)),
                   pl.BlockSpec((1, 8, 128), lambda h: (h, 0, 0))],
        out_shape=[jax.ShapeDtypeStruct((MOBA_HEADS, blk, blk), F32),
                   jax.ShapeDtypeStruct((MOBA_HEADS, blk, blk), F32),
                   jax.ShapeDtypeStruct((MOBA_HEADS, 8, past_len), F32),
                   jax.ShapeDtypeStruct((MOBA_HEADS, 8, 128), F32)],
        compiler_params=_params(1, 32),
        name="bias_tables",
    )(rel_bias)


def _in_proj_kernel(x_ref, g_ref, w_ref, u_ref, q_ref, k_ref, v_ref, qx_ref, gates_ref, kmean_ref):
    xb = _rmsnorm(x_ref[...], g_ref[...]).astype(BF16)

    def proj(lo, n):
        return jnp.dot(xb, w_ref[:, lo:lo + n], preferred_element_type=F32)

    u_ref[...] = proj(0, POOL_W)
    q_ref[...] = proj(POOL_W, ATT_W)
    k = proj(POOL_W + ATT_W, ATT_W)
    k_ref[...] = k
    for r in range(ROW_TILE // MOBA_BLOCK):
        kmean_ref[r] = jnp.mean(k[r * MOBA_BLOCK:(r + 1) * MOBA_BLOCK], axis=0, keepdims=True)
    v_ref[...] = proj(POOL_W + 2 * ATT_W, ATT_W)
    qx_ref[...] = proj(POOL_W + 3 * ATT_W, X_W)
    gates_ref[...] = jax.nn.sigmoid(proj(POOL_W + 3 * ATT_W + X_W, 3 * D_MODEL)).astype(gates_ref.dtype)


def _in_proj(x, norm1, w_in, layer):
    n = x.shape[0]
    in_cols = w_in.shape[-1]
    row = lambda w: pl.BlockSpec((ROW_TILE, w), lambda i: (i, 0))
    blocks_per_tile = ROW_TILE // MOBA_BLOCK
    return pl.pallas_call(
        _in_proj_kernel,
        grid=(n // ROW_TILE,),
        in_specs=[row(D_MODEL),
                  pl.BlockSpec((None, 1, D_MODEL), lambda i: (layer, 0, 0)),
                  pl.BlockSpec((None, D_MODEL, in_cols), lambda i: (layer, 0, 0))],
        out_specs=[row(POOL_W), row(ATT_W), row(ATT_W), row(ATT_W), row(X_W), row(3 * D_MODEL),
                   pl.BlockSpec((blocks_per_tile, 1, ATT_W), lambda i: (i, 0, 0))],
        out_shape=[jax.ShapeDtypeStruct((n, POOL_W), F32), jax.ShapeDtypeStruct((n, ATT_W), F32),
                   jax.ShapeDtypeStruct((n, ATT_W), F32), jax.ShapeDtypeStruct((n, ATT_W), F32),
                   jax.ShapeDtypeStruct((n, X_W), F32), jax.ShapeDtypeStruct((n, 3 * D_MODEL), BF16),
                   jax.ShapeDtypeStruct((n // MOBA_BLOCK, 1, ATT_W), F32)],
        compiler_params=_params(1, 48),
        name="in_proj",
    )(x, norm1, w_in)


def _mem_kv_kernel(x_ref, g_ref, w_ref, mk_ref, mv_ref):
    xb = _rmsnorm(x_ref[...], g_ref[...]).astype(BF16)
    mk_ref[...] = jnp.dot(xb, w_ref[:, :X_W], preferred_element_type=F32)
    mv_ref[...] = jnp.dot(xb, w_ref[:, X_W:], preferred_element_type=F32)


def _mem_kv(mem, mem_norm, w_mem_kv, layer):
    n = mem.shape[0]
    row = lambda w: pl.BlockSpec((ROW_TILE, w), lambda i: (i, 0))
    return pl.pallas_call(
        _mem_kv_kernel,
        grid=(n // ROW_TILE,),
        in_specs=[row(D_MODEL),
                  pl.BlockSpec((None, 1, D_MODEL), lambda i: (layer, 0, 0)),
                  pl.BlockSpec((None, D_MODEL, 2 * X_W), lambda i: (layer, 0, 0))],
        out_specs=[row(X_W), row(X_W)],
        out_shape=[jax.ShapeDtypeStruct((n, X_W), F32), jax.ShapeDtypeStruct((n, X_W), F32)],
        compiler_params=_params(1, 32),
        name="mem_kv",
    )(mem, mem_norm, w_mem_kv)


def _gate_augment(q_ref, k_ref, v_ref, kmt_ref, qa_ref, ka_ref, vt_ref, kt_all_ref, vt_all_ref, sel_ref, nb):
    i = pl.program_id(1)
    blk = MOBA_BLOCK
    qt = q_ref[...].T
    st = jnp.dot(kmt_ref[0], qt, precision=lax.Precision.HIGHEST,
                 preferred_element_type=F32)
    row_block = lax.broadcasted_iota(jnp.int32, st.shape, 0) // MOBA_HEADS
    st = jnp.where(row_block < i, st, -jnp.inf)
    slabs = [st[j * MOBA_HEADS:(j + 1) * MOBA_HEADS] for j in range(nb)]
    for j in range(nb):
        rank = jnp.zeros((MOBA_HEADS, blk), jnp.int32)
        for jp in range(nb):
            if jp == j:
                continue
            ahead = (slabs[jp] >= slabs[j]) if jp < j else (slabs[jp] > slabs[j])
            rank = rank + ahead.astype(jnp.int32)
        limit = jnp.where(j < i, MOBA_TOPK, jnp.where(j == i, nb + 1, 0))
        mask = jnp.where(rank < limit, 0.0, MASKED)
        for c in range(blk // 128):
            sel_ref[c, j * MOBA_HEADS:(j + 1) * MOBA_HEADS, :] = mask[:, c * 128:(c + 1) * 128]

    pad_rows = AUG_W - MOBA_HEAD_DIM - nb
    for h in range(MOBA_HEADS):
        pieces = [qt[h * MOBA_HEAD_DIM:(h + 1) * MOBA_HEAD_DIM] * (LOG2E * MOBA_HEAD_DIM ** -0.5),
                  jnp.concatenate([sel_ref[c, pl.ds(h, nb, stride=MOBA_HEADS), :]
                                   for c in range(blk // 128)], axis=1)]
        if pad_rows:
            pieces.append(jnp.zeros((pad_rows, blk), F32))
        qa_ref[0, h] = jnp.concatenate(pieces, axis=0).astype(BF16)

    lane = lax.broadcasted_iota(jnp.int32, (blk, AUG_W), 1)
    block_col = jnp.where(lane == MOBA_HEAD_DIM + i, 1.0, 0.0)
    k = k_ref[...]
    for pair in range(MOBA_HEADS // 2):
        x = k[:, pair * AUG_W:(pair + 1) * AUG_W]
        ka_ref[0, 2 * pair, 0] = jnp.where(lane < MOBA_HEAD_DIM, x, block_col).astype(BF16)
        ka_ref[0, 2 * pair + 1, 0] = jnp.where(lane < MOBA_HEAD_DIM, pltpu.roll(x, MOBA_HEAD_DIM, 1),
                                               block_col).astype(BF16)
    kt_all_ref[...] = k.T
    vt = v_ref[...].T
    vt_all_ref[...] = vt
    ones_row = jnp.where(lax.broadcasted_iota(jnp.int32, (VT_ROWS - MOBA_HEAD_DIM, blk), 0) == 0, 1.0, 0.0)
    for h in range(MOBA_HEADS):
        vt_ref[0, h, 0] = jnp.concatenate([vt[h * MOBA_HEAD_DIM:(h + 1) * MOBA_HEAD_DIM], ones_row],
                                          axis=0).astype(BF16)


def _moba_kernel(qa_ref, ka_ref, vt_ref, own_ref, prev_ref, o_ref, m_ref, acc_ref):
    i = pl.program_id(1)

    def scores(h, j, bias):
        s = jnp.dot(ka_ref[0, h, j], qa_ref[0, h], preferred_element_type=F32)
        return s if bias is None else s + bias[h]

    def fold(h, j, s, first):
        m_blk = jnp.max(s, axis=0, keepdims=True)
        m_new = m_blk if first else jnp.maximum(m_ref[h], m_blk)
        p = jnp.exp2(s - m_new)
        pv = jnp.dot(vt_ref[0, h, j], p.astype(BF16), preferred_element_type=F32)
        if first:
            acc_ref[h] = pv
        else:
            acc_ref[h] = jnp.exp2(m_ref[h] - m_new) * acc_ref[h] + pv
        m_ref[h] = m_new

    def blocks(*specs):
        items = [(h, j, bias, first) for (j, bias, first) in specs for h in range(MOBA_HEADS)]
        pending = [scores(h, j, bias) for (h, j, bias, _) in items[:LOOKAHEAD]]
        for n, (h, j, _, first) in enumerate(items):
            s = pending.pop(0)
            if n + LOOKAHEAD < len(items):
                nh, nj, nbias, _ = items[n + LOOKAHEAD]
                pending.append(scores(nh, nj, nbias))
            fold(h, j, s, first)

    @pl.when(i == 0)
    def _():
        blocks((i, own_ref, True))

    @pl.when(i >= 1)
    def _():
        blocks((i, own_ref, True), (i - 1, prev_ref, False))

    n_far = jnp.maximum(i - 1, 0)

    def far_pair(t, carry):
        blocks((2 * t, None, False), (2 * t + 1, None, False))
        return carry

    lax.fori_loop(0, n_far // 2, far_pair, 0)

    @pl.when(n_far % 2 == 1)
    def _():
        blocks((n_far - 1, None, False))

    for pair in range(MOBA_HEADS // 2):
        outs = [acc_ref[h, :MOBA_HEAD_DIM] * (1.0 / acc_ref[h, MOBA_HEAD_DIM:MOBA_HEAD_DIM + 1])
                for h in (2 * pair, 2 * pair + 1)]
        o_ref[:, pair * AUG_W:(pair + 1) * AUG_W] = jnp.concatenate(outs, axis=0).T.astype(o_ref.dtype)


def _moba_prompt(qa, ka, vt, bias_own, bias_prev, batch, seq):
    nb = seq // MOBA_BLOCK
    blk = MOBA_BLOCK
    return pl.pallas_call(
        _moba_kernel,
        grid=(batch, nb),
        in_specs=[pl.BlockSpec((1, MOBA_HEADS, AUG_W, blk), lambda b, i: (b, 0, 0, i)),
                  pl.BlockSpec((1, MOBA_HEADS, nb, blk, AUG_W), lambda b, i: (b, 0, 0, 0, 0)),
                  pl.BlockSpec((1, MOBA_HEADS, nb, VT_ROWS, blk), lambda b, i: (b, 0, 0, 0, 0)),
                  pl.BlockSpec((MOBA_HEADS, blk, blk), lambda b, i: (0, 0, 0)),
                  pl.BlockSpec((MOBA_HEADS, blk, blk), lambda b, i: (0, 0, 0))],
        out_specs=pl.BlockSpec((blk, ATT_W), lambda b, i: (b * nb + i, 0)),
        out_shape=jax.ShapeDtypeStruct((batch * seq, ATT_W), BF16),
        scratch_shapes=[pltpu.VMEM((MOBA_HEADS, 1, blk), F32), pltpu.VMEM((MOBA_HEADS, VT_ROWS, blk), F32)],
        compiler_params=_params(2, 48),
        name="moba_prompt",
    )(qa, ka, vt, bias_own, bias_prev)


def _window_means(pad_ref, base, rows, inv_count):
    groups = []
    for g, w in enumerate(POOL_WINDOWS):
        cols = slice(g * POOL_GROUP, (g + 1) * POOL_GROUP)
        tok = pad_ref[base:base + rows, cols]
        acc = tok
        for s in range(1, w):
            acc = acc + pad_ref[base - s:base - s + rows, cols]
        groups.append(acc * inv_count(w) - tok)
    return jnp.concatenate(groups, axis=1)


def _cross_attend(qx, mk, mv):
    outs = []
    for h in range(X_HEADS):
        cols = slice(h * X_HEAD_DIM, (h + 1) * X_HEAD_DIM)
        s = _bdot_nt(qx[:, cols], mk[:, cols]) * (X_HEAD_DIM ** -0.5)
        m = jnp.max(s, axis=-1, keepdims=True)
        p = jnp.exp(s - m)
        l = jnp.sum(p, axis=-1, keepdims=True)
        outs.append(_bdot(p, mv[:, cols]) * (1.0 / l))
    return jnp.concatenate(outs, axis=1)


def _pool_and_cross(u_ref, halo_ref, qx_ref, mk_ref, mv_ref, pooled_ref, c_ref, pad_ref):
    i = pl.program_id(1)
    rows = ROW_TILE
    base = 16
    first = jnp.full((base, 1), i, jnp.int32) == 0
    pad_ref[0:base, :] = jnp.where(first, 0.0, halo_ref[...])
    pad_ref[base:base + rows, :] = u_ref[...]
    pos = i * rows + lax.broadcasted_iota(jnp.int32, (rows, 1), 0)

    def inv_count(w):
        return 1.0 / jnp.minimum(pos + 1, w).astype(F32)

    pooled_ref[...] = _window_means(pad_ref, base, rows, inv_count).astype(pooled_ref.dtype)
    c_ref[...] = _cross_attend(qx_ref[...], mk_ref[...].astype(BF16), mv_ref[...].astype(BF16)).astype(c_ref.dtype)


def _prompt_prep_kernel(*refs, nb, aliased):
    n_in = 11 if aliased else 9
    q_ref, k_ref, v_ref, kmt_ref, u_ref, halo_ref, qx_ref, mk_ref, mv_ref = refs[:9]
    qa_ref, ka_ref, vt_ref, kt_all_ref, vt_all_ref, pooled_ref, c_ref, sel_ref, pad_ref = refs[n_in:]
    _gate_augment(q_ref, k_ref, v_ref, kmt_ref, qa_ref, ka_ref, vt_ref, kt_all_ref, vt_all_ref, sel_ref, nb)
    _pool_and_cross(u_ref, halo_ref, qx_ref, mk_ref, mv_ref, pooled_ref, c_ref, pad_ref)


def _prompt_prep(q, k, v, kmt, u, qx, mk, mv, kv_all, layer, depth, batch, seq, mem_len):
    nb = seq // MOBA_BLOCK
    blk = MOBA_BLOCK
    n = batch * seq
    assert ROW_TILE == blk
    halo_per_tile = blk // 16
    row = pl.BlockSpec((blk, ATT_W), lambda b, i: (b * nb + i, 0))
    halo = pl.BlockSpec((16, POOL_W), lambda b, i: (jnp.maximum((b * nb + i) * halo_per_tile - 1, 0), 0))
    mem = pl.BlockSpec((mem_len, X_W), lambda b, i: (b, 0))
    slab = pl.BlockSpec((None, None, ATT_W, blk), lambda b, i: (layer, b, 0, i))
    in_specs = [row, row, row, pl.BlockSpec((1, nb * MOBA_HEADS, ATT_W), lambda b, i: (b, 0, 0)),
                row, halo, row, mem, mem]
    args = [q, k, v, kmt, u, u, qx, mk, mv]
    aliases = {}
    if kv_all is not None:
        in_specs += [pl.BlockSpec(memory_space=pl.ANY)] * 2
        args += list(kv_all)
        aliases = {9: 3, 10: 4}
    kv_shape = jax.ShapeDtypeStruct((depth, batch, ATT_W, seq), F32)
    return pl.pallas_call(
        functools.partial(_prompt_prep_kernel, nb=nb, aliased=kv_all is not None),
        grid=(batch, nb),
        in_specs=in_specs,
        out_specs=[pl.BlockSpec((1, MOBA_HEADS, AUG_W, blk), lambda b, i: (b, 0, 0, i)),
                   pl.BlockSpec((1, MOBA_HEADS, 1, blk, AUG_W), lambda b, i: (b, 0, i, 0, 0)),
                   pl.BlockSpec((1, MOBA_HEADS, 1, VT_ROWS, blk), lambda b, i: (b, 0, i, 0, 0)),
                   slab, slab, row, row],
        out_shape=[jax.ShapeDtypeStruct((batch, MOBA_HEADS, AUG_W, seq), BF16),
                   jax.ShapeDtypeStruct((batch, MOBA_HEADS, nb, blk, AUG_W), BF16),
                   jax.ShapeDtypeStruct((batch, MOBA_HEADS, nb, VT_ROWS, blk), BF16),
                   kv_shape, kv_shape,
                   jax.ShapeDtypeStruct((n, POOL_W), BF16), jax.ShapeDtypeStruct((n, X_W), BF16)],
        input_output_aliases=aliases,
        scratch_shapes=[pltpu.VMEM((blk // 128, nb * MOBA_HEADS, 128), F32),
                        pltpu.VMEM((16 + blk, POOL_W), F32)],
        compiler_params=_params(2, 32),
        name="prompt_prep",
    )(*args)


def _sample_kernel(pt_ref, *refs, n_pages, dec_seq):
    seqs = SAMPLE_SEQS
    k_pages = refs[:seqs * n_pages]
    v_pages = refs[seqs * n_pages:2 * seqs * n_pages]
    (q_ref, k_ref, v_ref, u_ref, qx_ref, mk_ref, mv_ref, buf_ref, spast_ref, sown_ref,
     pooled_ref, b_ref, c_ref, pad_ref) = refs[2 * seqs * n_pages:]
    del pt_ref
    toks = seqs * dec_seq
    rows = MOBA_HEADS * toks
    pages_per_block = MOBA_BLOCK // PAGE
    nb = n_pages // pages_per_block
    tok_seq = lax.broadcasted_iota(jnp.int32, (toks, 1), 0) // dec_seq
    col_head = lax.broadcasted_iota(jnp.int32, (toks, ATT_W), 1) // MOBA_HEAD_DIM

    q_tok = q_ref[...] * (MOBA_HEAD_DIM ** -0.5)
    qbd = jnp.concatenate([jnp.where(col_head == h, q_tok, 0.0) for h in range(MOBA_HEADS)], axis=0)
    q_hi = qbd.astype(BF16)
    q_lo = (qbd - q_hi.astype(F32)).astype(BF16)
    q_split = jnp.concatenate([q_hi, q_lo], axis=0)
    k_new = k_ref[...]
    v_new = v_ref[...]
    b_out = jnp.zeros((toks, ATT_W), F32)
    scored = [_sample_scores(q_split, k_pages[a * n_pages:(a + 1) * n_pages], nb) for a in range(seqs)]
    for a in range(seqs):
        b_seq = _sample_attend(*scored[a], qbd, v_pages[a * n_pages:(a + 1) * n_pages],
                               k_new[a * dec_seq:(a + 1) * dec_seq], v_new[a * dec_seq:(a + 1) * dec_seq],
                               spast_ref, sown_ref)
        merged = functools.reduce(
            jnp.add, [jnp.where(col_head == h, b_seq[h * toks:(h + 1) * toks], 0.0) for h in range(MOBA_HEADS)])
        b_out = jnp.where(tok_seq == a, merged, b_out)
    b_ref[...] = b_out

    qx_tok = qx_ref[...] * (X_HEAD_DIM ** -0.5)
    qx_rows = jnp.concatenate([qx_tok[:, h * X_HEAD_DIM:(h + 1) * X_HEAD_DIM] for h in range(X_HEADS)],
                              axis=0)
    mem_rows = mk_ref.shape[1]
    same_head = (lax.broadcasted_iota(jnp.int32, (X_HEADS * toks, mem_rows), 1) % X_HEADS
                 == lax.broadcasted_iota(jnp.int32, (X_HEADS * toks, mem_rows), 0) // toks)
    c_out = jnp.zeros((toks, X_W), F32)
    for a in range(seqs):
        s = jnp.where(same_head, _bdot_nt(qx_rows, mk_ref[a]), MASKED)
        m = jnp.max(s, axis=-1, keepdims=True)
        pr = jnp.exp(s - m)
        l = jnp.sum(pr, axis=-1, keepdims=True)
        o = _bdot(pr, mv_ref[a]) * (1.0 / l)
        merged = jnp.concatenate([o[h * toks:(h + 1) * toks] for h in range(X_HEADS)], axis=1)
        c_out = jnp.where(tok_seq == a, merged, c_out)
    c_ref[...] = c_out

    base = 16
    u_tok = u_ref[...]
    for a in range(seqs):
        pad = pad_ref.at[a]
        pad[base - POOL_STATE:base, :] = buf_ref[a]
        pad[base:base + dec_seq, :] = u_tok[a * dec_seq:(a + 1) * dec_seq]
        pooled_ref[a * dec_seq:(a + 1) * dec_seq, :] = _window_means(pad, base, dec_seq, lambda w: 1.0 / w)


def _sample_scores(q_split, k_pages, nb):
    rows = q_split.shape[0] // 2
    pages_per_block = len(k_pages) // nb
    raw, gate = [], []
    for p, page in enumerate(k_pages):
        s = jnp.dot(q_split, page[...].astype(BF16), preferred_element_type=F32)
        raw.append(s[:rows])
        part = jnp.sum(s[:rows] + s[rows:], axis=-1, keepdims=True)
        if p % pages_per_block == 0:
            gate.append(part)
        else:
            gate[-1] = gate[-1] + part
    return raw, gate


def _sample_attend(raw, gate, qbd, v_pages, k_new, v_new, spast_ref, sown_ref):
    rows = qbd.shape[0]
    n_pages = len(v_pages)
    nb = len(gate)
    dec_seq = k_new.shape[0]
    pages_per_block = n_pages // nb
    logits = []
    for p in range(n_pages):
        j = p // pages_per_block
        rank = jnp.zeros((rows, 1), jnp.int32)
        if p % pages_per_block == 0:
            for jp in range(nb):
                if jp != j:
                    ahead = (gate[jp] >= gate[j]) if jp < j else (gate[jp] > gate[j])
                    rank = rank + ahead.astype(jnp.int32)
            block_mask = jnp.where(rank < MOBA_TOPK, 0.0, MASKED)
        logits.append(raw[p] + spast_ref[:, p * PAGE:(p + 1) * PAGE] + block_mask)
    own = [jnp.sum(qbd * k_new[t:t + 1], axis=-1, keepdims=True) + sown_ref[:, t:t + 1]
           for t in range(dec_seq)]
    m = functools.reduce(jnp.maximum, [jnp.max(s, axis=-1, keepdims=True) for s in logits] + own)
    l = jnp.zeros((rows, 1), F32)
    o = jnp.zeros((rows, ATT_W), F32)
    for p in range(n_pages):
        pr = jnp.exp(logits[p] - m)
        l = l + jnp.sum(pr, axis=-1, keepdims=True)
        o = o + _bdot_nt(pr, v_pages[p][...])
    for t in range(dec_seq):
        pr = jnp.exp(own[t] - m)
        l = l + pr
        o = o + pr * v_new[t:t + 1]
    return o * (1.0 / l)


def _sample_mixers(page_table, cache_k, cache_v, q, k, v, u, qx, mem_k, mem_v, pool_buf, spast, sown, layer,
                   n_prompt):
    dec_batch, n_pages = page_table.shape
    seqs = SAMPLE_SEQS
    dec_seq = (q.shape[0] - n_prompt) // dec_batch
    toks = seqs * dec_seq
    rows = toks * MOBA_HEADS
    mem_rows = mem_k.shape[2]

    def page_spec(a, p):
        return pl.BlockSpec((None, None, ATT_W, PAGE),
                            lambda b, pt: (layer, pt[(b * seqs + a) * n_pages + p], 0, 0))

    pages = [page_spec(a, p) for a in range(seqs) for p in range(n_pages)]
    tok = lambda w: pl.BlockSpec((toks, w), lambda b, pt: (n_prompt // toks + b, 0))
    mem = pl.BlockSpec((None, seqs, mem_rows, X_HEAD_DIM), lambda b, pt: (layer, b, 0, 0))
    in_specs = (pages * 2
                + [tok(ATT_W), tok(ATT_W), tok(ATT_W), tok(POOL_W), tok(X_W), mem, mem,
                   pl.BlockSpec((None, seqs, POOL_STATE, POOL_W), lambda b, pt: (layer, b, 0, 0)),
                   pl.BlockSpec((rows, n_pages * PAGE), lambda b, pt: (0, 0)),
                   pl.BlockSpec((rows, 128), lambda b, pt: (0, 0))])
    out_spec = lambda w: pl.BlockSpec((toks, w), lambda b, pt: (b, 0))
    out = lambda w: jax.ShapeDtypeStruct((dec_batch * dec_seq, w), F32)
    return pl.pallas_call(
        functools.partial(_sample_kernel, n_pages=n_pages, dec_seq=dec_seq),
        grid_spec=pltpu.PrefetchScalarGridSpec(
            num_scalar_prefetch=1,
            grid=(dec_batch // seqs,),
            in_specs=in_specs,
            out_specs=[out_spec(POOL_W), out_spec(ATT_W), out_spec(X_W)],
            scratch_shapes=[pltpu.VMEM((seqs, 16 + 8, POOL_W), F32)]),
        out_shape=[out(POOL_W), out(ATT_W), out(X_W)],
        compiler_params=_params(1, 52),
        name="sample_mixers",
    )(page_table.reshape(-1), *([cache_k] * (seqs * n_pages)), *([cache_v] * (seqs * n_pages)),
      q, k, v, u, qx, mem_k, mem_v, pool_buf, spast, sown)


def _merge_kernel(x_ref, pooled_p_ref, pooled_s_ref, b_p_ref, b_s_ref, c_p_ref, c_s_ref, gates_ref,
                  pw_ref, ps_ref, wa_ref, wb_ref, wc_ref, wo_ref, o_ref, *, prompt_tiles):
    is_sample = jnp.full((ROW_TILE, 1), pl.program_id(0), jnp.int32) >= prompt_tiles
    pooled = jnp.where(is_sample, pooled_s_ref[...].astype(BF16), pooled_p_ref[...])
    b_out = jnp.where(is_sample, b_s_ref[...].astype(BF16), b_p_ref[...])
    c_out = jnp.where(is_sample, c_s_ref[...].astype(BF16), c_p_ref[...])
    mixed = jnp.concatenate(
        [_bdot(pooled[:, g * POOL_GROUP:(g + 1) * POOL_GROUP], pw_ref[g]) for g in range(len(POOL_WINDOWS))],
        axis=1) * ps_ref[...]
    merged = (gates_ref[:, :D_MODEL] * _bdot(mixed, wa_ref[...])
              + gates_ref[:, D_MODEL:2 * D_MODEL] * _bdot(b_out, wb_ref[...])
              + gates_ref[:, 2 * D_MODEL:] * _bdot(c_out, wc_ref[...]))
    o_ref[...] = x_ref[...] + _bdot(merged, wo_ref[...])


def _merge(x, pooled, b_out, c_out, gates, pool_w, pool_scale, w_br_a, w_br_b, w_br_c, w_out, layer):
    n = x.shape[0]
    prompt_tiles = pooled[0].shape[0] // ROW_TILE
    row = lambda w: pl.BlockSpec((ROW_TILE, w), lambda i: (i, 0))
    prompt = lambda w: pl.BlockSpec((ROW_TILE, w), lambda i: (jnp.minimum(i, prompt_tiles - 1), 0))
    sample = lambda w: pl.BlockSpec((ROW_TILE, w), lambda i: (jnp.maximum(i - prompt_tiles, 0), 0))
    wspec = lambda *shape: pl.BlockSpec((None,) + shape, lambda i: (layer,) + (0,) * len(shape))
    return pl.pallas_call(
        functools.partial(_merge_kernel, prompt_tiles=prompt_tiles),
        grid=(n // ROW_TILE,),
        in_specs=[row(D_MODEL), prompt(POOL_W), sample(POOL_W), prompt(ATT_W), sample(ATT_W),
                  prompt(X_W), sample(X_W), row(3 * D_MODEL),
                  wspec(len(POOL_WINDOWS), POOL_GROUP, POOL_GROUP), wspec(1, POOL_W),
                  wspec(POOL_W, D_MODEL), wspec(ATT_W, D_MODEL), wspec(X_W, D_MODEL), wspec(D_MODEL, D_MODEL)],
        out_specs=row(D_MODEL),
        out_shape=jax.ShapeDtypeStruct((n, D_MODEL), F32),
        compiler_params=_params(1, 48),
        name="merge",
    )(x, pooled[0], pooled[1], b_out[0], b_out[1], c_out[0], c_out[1], gates,
      pool_w, pool_scale, w_br_a, w_br_b, w_br_c, w_out)


def _ffn_kernel(x_ref, g_ref, wi_ref, wo_ref, o_ref, *, d_ff):
    x = x_ref[...]
    xb = _rmsnorm(x, g_ref[...]).astype(BF16)
    gate = jnp.dot(xb, wi_ref[:, :d_ff], preferred_element_type=F32)
    up = jnp.dot(xb, wi_ref[:, d_ff:], preferred_element_type=F32)
    act = gate * jax.nn.sigmoid(gate) * up
    o_ref[...] = x + _bdot(act, wo_ref[...])


def _ffn(x, norm2, w_ffn_in, w_ffn_out, layer):
    n = x.shape[0]
    d_ff = w_ffn_out.shape[1]
    row = pl.BlockSpec((ROW_TILE, D_MODEL), lambda i: (i, 0))
    return pl.pallas_call(
        functools.partial(_ffn_kernel, d_ff=d_ff),
        grid=(n // ROW_TILE,),
        in_specs=[row,
                  pl.BlockSpec((None, 1, D_MODEL), lambda i: (layer, 0, 0)),
                  pl.BlockSpec((None, D_MODEL, 2 * d_ff), lambda i: (layer, 0, 0)),
                  pl.BlockSpec((None, d_ff, D_MODEL), lambda i: (layer, 0, 0))],
        out_specs=row,
        out_shape=jax.ShapeDtypeStruct((n, D_MODEL), F32),
        compiler_params=_params(1, 56),
        name="ffn",
    )(x, norm2, w_ffn_in, w_ffn_out)


def _final_norm_kernel(x_ref, g_ref, o_ref):
    o_ref[...] = _rmsnorm(x_ref[...], g_ref[...])


def _final_norm(x, g, first_row, n_rows):
    first_tile = first_row // ROW_TILE
    return pl.pallas_call(
        _final_norm_kernel,
        grid=(n_rows // ROW_TILE,),
        in_specs=[pl.BlockSpec((ROW_TILE, D_MODEL), lambda i: (first_tile + i, 0)),
                  pl.BlockSpec((1, D_MODEL), lambda i: (0, 0))],
        out_specs=pl.BlockSpec((ROW_TILE, D_MODEL), lambda i: (i, 0)),
        out_shape=jax.ShapeDtypeStruct((n_rows, D_MODEL), F32),
        compiler_params=_params(1, 32),
        name="final_norm",
    )(x, g)


def kernel(x_prompt, x_sample, cache_attn_k, cache_attn_v, cache_mem_k, cache_mem_v, state_pool, page_table, mem_prompt, norm1, w_in, pool_w, pool_scale, mem_norm, w_mem_kv, w_br_a, w_br_b, w_br_c, w_out, norm2, w_ffn_in, w_ffn_out, rel_bias, final_norm):
    batch, seq, d = x_prompt.shape
    dec_batch, dec_seq = x_sample.shape[:2]
    depth = w_in.shape[0]
    n_phys = cache_attn_k.shape[1]
    n_pages = page_table.shape[1]
    mem_len = mem_prompt.shape[1]
    past_len = n_pages * PAGE
    nb = seq // MOBA_BLOCK
    n_prompt = batch * seq
    n_sample = dec_batch * dec_seq
    assert d == D_MODEL and cache_attn_k.shape[2] == PAGE
    assert seq % ROW_TILE == 0 and n_sample % ROW_TILE == 0 and nb % 8 == 0 and nb <= AUG_W - MOBA_HEAD_DIM
    assert past_len % MOBA_BLOCK == 0 and past_len // MOBA_BLOCK >= MOBA_TOPK
    assert SAMPLE_SEQS * dec_seq == 8 and dec_batch % SAMPLE_SEQS == 0
    assert (batch * mem_len) % ROW_TILE == 0 and ROW_TILE % MOBA_BLOCK == 0

    bf = lambda w: w.astype(BF16)
    w_in_b, w_mem_b, pool_w_b = bf(w_in), bf(w_mem_kv), bf(pool_w)
    w_a, w_b, w_c, w_o, w_fi, w_fo = bf(w_br_a), bf(w_br_b), bf(w_br_c), bf(w_out), bf(w_ffn_in), bf(w_ffn_out)
    norm1_r, norm2_r, mem_norm_r = (g.reshape(depth, 1, D_MODEL) for g in (norm1, norm2, mem_norm))
    pool_scale_r = pool_scale.reshape(depth, 1, POOL_W)
    cache_k = cache_attn_k.transpose(0, 1, 3, 4, 2).reshape(depth, n_phys, ATT_W, PAGE)
    cache_v = cache_attn_v.transpose(0, 1, 3, 4, 2).reshape(depth, n_phys, ATT_W, PAGE)
    mem_k_s = cache_mem_k.reshape(depth, dec_batch, mem_len * X_HEADS, X_HEAD_DIM)
    mem_v_s = cache_mem_v.reshape(depth, dec_batch, mem_len * X_HEADS, X_HEAD_DIM)
    mem_rows = mem_prompt.reshape(batch * mem_len, D_MODEL)

    bias_own, bias_prev, spast, sown = _bias_tables(rel_bias, past_len)

    def sample_table(t):
        t = jnp.broadcast_to(t[:, None, :dec_seq], (MOBA_HEADS, SAMPLE_SEQS, dec_seq, t.shape[-1]))
        return t.reshape(MOBA_HEADS * SAMPLE_SEQS * dec_seq, t.shape[-1])

    spast, sown = sample_table(spast), sample_table(sown)
    head_of_col = jnp.arange(ATT_W) // MOBA_HEAD_DIM

    x = jnp.concatenate([x_prompt.reshape(n_prompt, D_MODEL), x_sample.reshape(n_sample, D_MODEL)], axis=0)
    kv_all = None
    mkp_l, mvp_l, bp_l, ks_l, vs_l, bs_l = [], [], [], [], [], []
    for layer in range(depth):
        u, q, k, v, qx, gates, kmean = _in_proj(x, norm1_r, w_in_b, layer)
        mk, mv = _mem_kv(mem_rows, mem_norm_r, w_mem_b, layer)

        km = kmean[:n_prompt // MOBA_BLOCK, 0].reshape(batch, nb, 1, ATT_W)
        kmt = jnp.where(jnp.arange(MOBA_HEADS)[None, None, :, None] == head_of_col[None, None, None, :], km, 0.0)
        kmt = kmt.reshape(batch, nb * MOBA_HEADS, ATT_W)
        qa, ka, vt, kt_all, vt_all, pooled_p, c_prompt = _prompt_prep(
            q, k, v, kmt, u, qx, mk, mv, kv_all, layer, depth, batch, seq, mem_len)
        kv_all = (kt_all, vt_all)
        b_prompt = _moba_prompt(qa, ka, vt, bias_own, bias_prev, batch, seq)

        pooled_s, b_sample, c_sample = _sample_mixers(
            page_table, cache_k, cache_v, q, k, v, u, qx, mem_k_s, mem_v_s, state_pool, spast, sown, layer, n_prompt)

        x = _merge(x, (pooled_p, pooled_s), (b_prompt, b_sample), (c_prompt, c_sample), gates,
                   pool_w_b, pool_scale_r, w_a, w_b, w_c, w_o, layer)
        x = _ffn(x, norm2_r, w_fi, w_fo, layer)

        tok = lambda a: a[n_prompt:].reshape(dec_batch, dec_seq, a.shape[-1])
        mkp_l.append(mk.reshape(batch, mem_len, X_HEADS, X_HEAD_DIM))
        mvp_l.append(mv.reshape(batch, mem_len, X_HEADS, X_HEAD_DIM))
        bp_l.append(jnp.stack([u[(b + 1) * seq - POOL_STATE:(b + 1) * seq] for b in range(batch)]))
        ks_l.append(tok(k).reshape(dec_batch, dec_seq, MOBA_HEADS, MOBA_HEAD_DIM))
        vs_l.append(tok(v).reshape(dec_batch, dec_seq, MOBA_HEADS, MOBA_HEAD_DIM))
        bs_l.append(jnp.concatenate([state_pool[layer], tok(u)], axis=1)[:, -POOL_STATE:])

    g_final = final_norm.reshape(1, D_MODEL)
    y_prompt = _final_norm(x, g_final, 0, n_prompt)
    y_sample = _final_norm(x, g_final, n_prompt, n_sample)
    heads_last = lambda t: t.reshape(depth, batch, MOBA_HEADS, MOBA_HEAD_DIM, seq).transpose(0, 1, 4, 2, 3)
    return (y_prompt.reshape(batch, seq, D_MODEL), y_sample.reshape(dec_batch, dec_seq, D_MODEL),
            heads_last(kv_all[0]), heads_last(kv_all[1]), jnp.stack(mkp_l), jnp.stack(mvp_l),
            jnp.stack(bp_l), jnp.stack(ks_l), jnp.stack(vs_l), jnp.stack(bs_l))
```

```python
import functools
import math

import jax
import jax.numpy as jnp
from jax import lax
from jax.experimental import pallas as pl
from jax.experimental.pallas import tpu as pltpu

F32 = jnp.float32
BF16 = jnp.bfloat16

D_MODEL = 1024
POOL_W = 512
POOL_WINDOWS = (2, 4, 8, 16)
POOL_GROUP = POOL_W // len(POOL_WINDOWS)
POOL_STATE = max(POOL_WINDOWS) - 1
MOBA_HEADS = 8
MOBA_HEAD_DIM = 64
ATT_W = MOBA_HEADS * MOBA_HEAD_DIM
MOBA_BLOCK = 256
MOBA_TOPK = 3
X_HEADS = 4
X_HEAD_DIM = 128
X_W = X_HEADS * X_HEAD_DIM
N_BUCKETS = 32
MAX_EXACT = N_BUCKETS // 2
MAX_DISTANCE = 128
EPS = 1e-6
PAGE = 128

ROW_TILE = 256
MIX_TILE = 512
AUG_W = 128
VT_ROWS = 80
LOOKAHEAD = 6
FAR_RUN = 4
SAMPLE_SEQS = 2
MASKED = -1e30
LOG2E = math.log2(math.e)
MIB = 1 << 20


def _params(n_axes, vmem_mib):
    return pltpu.CompilerParams(dimension_semantics=("arbitrary",) * n_axes,
                                vmem_limit_bytes=vmem_mib * MIB)


def _rmsnorm(x, g):
    return x * lax.rsqrt(jnp.mean(x * x, axis=-1, keepdims=True) + EPS) * g


def _bdot(a, b):
    return jnp.dot(a.astype(BF16), b.astype(BF16), preferred_element_type=F32)


def _bdot_nt(a, b):
    return lax.dot_general(a.astype(BF16), b.astype(BF16), (((1,), (1,)), ((), ())),
                           preferred_element_type=F32)


def _bucket(rel):
    n = jnp.maximum(rel, 0)
    nf = jnp.maximum(n, 1).astype(F32)
    large = MAX_EXACT + (jnp.log(nf / MAX_EXACT) / math.log(MAX_DISTANCE / MAX_EXACT)
                         * (N_BUCKETS - MAX_EXACT)).astype(jnp.int32)
    large = jnp.minimum(large, N_BUCKETS - 1)
    return jnp.where(n < MAX_EXACT, n, large)


def _bias_lookup(rel, rb_ref, h):
    bucket = _bucket(rel)
    last = rb_ref[N_BUCKETS - 1, h]
    out = jnp.zeros(rel.shape, F32)
    for b in range(N_BUCKETS - 1):
        out = jnp.where(bucket == b, rb_ref[b, h] - last, out)
    return out


def _bias_table_kernel(rb_ref, own_ref, prev_ref, spast_ref, sown_ref, *, past_len):
    h = pl.program_id(0)
    blk = MOBA_BLOCK
    tk = lax.broadcasted_iota(jnp.int32, (blk, blk), 0)
    tq = lax.broadcasted_iota(jnp.int32, (blk, blk), 1)
    rel = tq - tk
    own_ref[0] = jnp.where(rel >= 0, _bias_lookup(rel, rb_ref, h) * LOG2E, MASKED)
    prev_ref[0] = _bias_lookup(rel + blk, rb_ref, h) * LOG2E
    t = lax.broadcasted_iota(jnp.int32, (8, past_len), 0)
    key = lax.broadcasted_iota(jnp.int32, (8, past_len), 1)
    spast_ref[0] = _bias_lookup(past_len + t - key, rb_ref, h)
    t = lax.broadcasted_iota(jnp.int32, (8, 128), 0)
    key = lax.broadcasted_iota(jnp.int32, (8, 128), 1)
    sown_ref[0] = jnp.where(key <= t, _bias_lookup(t - key, rb_ref, h), MASKED)


def _bias_tables(rel_bias, past_len):
    blk = MOBA_BLOCK
    return pl.pallas_call(
        functools.partial(_bias_table_kernel, past_len=past_len),
        grid=(MOBA_HEADS,),
        in_specs=[pl.BlockSpec(memory_space=pltpu.SMEM)],
        out_specs=[pl.BlockSpec((1, blk, blk), lambda h: (h, 0, 0)),
                   pl.BlockSpec((1, blk, blk), lambda h: (h, 0, 0)),
                   pl.BlockSpec((1, 8, past_len), lambda h: (h, 0, 0)),
                   pl.BlockSpec((1, 8, 128), lambda h: (h, 0, 0))],
        out_shape=[jax.ShapeDtypeStruct((MOBA_HEADS, blk, blk), F32),
                   jax.ShapeDtypeStruct((MOBA_HEADS, blk, blk), F32),
                   jax.ShapeDtypeStruct((MOBA_HEADS, 8, past_len), F32),
                   jax.ShapeDtypeStruct((MOBA_HEADS, 8, 128), F32)],
        compiler_params=_params(1, 32),
        name="bias_tables",
    )(rel_bias)


def _in_proj_kernel(x_ref, g_ref, w_ref, u_ref, q_ref, k_ref, v_ref, qx_ref, gates_ref, kmean_ref):
    xb = _rmsnorm(x_ref[...], g_ref[...]).astype(BF16)

    def proj(lo, n):
        return jnp.dot(xb, w_ref[:, lo:lo + n], preferred_element_type=F32)

    u_ref[...] = proj(0, POOL_W)
    q_ref[...] = proj(POOL_W, ATT_W)
    k = proj(POOL_W + ATT_W, ATT_W)
    k_ref[...] = k
    for r in range(ROW_TILE // MOBA_BLOCK):
        kmean_ref[r] = jnp.mean(k[r * MOBA_BLOCK:(r + 1) * MOBA_BLOCK], axis=0, keepdims=True)
    v_ref[...] = proj(POOL_W + 2 * ATT_W, ATT_W)
    qx_ref[...] = proj(POOL_W + 3 * ATT_W, X_W)
    gates_ref[...] = jax.nn.sigmoid(proj(POOL_W + 3 * ATT_W + X_W, 3 * D_MODEL)).astype(gates_ref.dtype)


def _in_proj(x, norm1, w_in, layer):
    n = x.shape[0]
    in_cols = w_in.shape[-1]
    row = lambda w: pl.BlockSpec((ROW_TILE, w), lambda i: (i, 0))
    blocks_per_tile = ROW_TILE // MOBA_BLOCK
    return pl.pallas_call(
        _in_proj_kernel,
        grid=(n // ROW_TILE,),
        in_specs=[row(D_MODEL),
                  pl.BlockSpec((None, 1, D_MODEL), lambda i: (layer, 0, 0)),
                  pl.BlockSpec((None, D_MODEL, in_cols), lambda i: (layer, 0, 0), pipeline_mode=pl.Buffered(1))],
        out_specs=[row(POOL_W), row(ATT_W), row(ATT_W), row(ATT_W), row(X_W), row(3 * D_MODEL),
                   pl.BlockSpec((blocks_per_tile, 1, ATT_W), lambda i: (i, 0, 0))],
        out_shape=[jax.ShapeDtypeStruct((n, POOL_W), F32), jax.ShapeDtypeStruct((n, ATT_W), F32),
                   jax.ShapeDtypeStruct((n, ATT_W), F32), jax.ShapeDtypeStruct((n, ATT_W), F32),
                   jax.ShapeDtypeStruct((n, X_W), F32), jax.ShapeDtypeStruct((n, 3 * D_MODEL), BF16),
                   jax.ShapeDtypeStruct((n // MOBA_BLOCK, 1, ATT_W), F32)],
        compiler_params=_params(1, 48),
        name="in_proj",
    )(x, norm1, w_in)


def _mem_kv_kernel(x_ref, g_ref, w_ref, mk_ref, mv_ref):
    xb = _rmsnorm(x_ref[...], g_ref[...]).astype(BF16)
    mk_ref[...] = jnp.dot(xb, w_ref[:, :X_W], preferred_element_type=F32)
    mv_ref[...] = jnp.dot(xb, w_ref[:, X_W:], preferred_element_type=F32)


def _mem_kv(mem, mem_norm, w_mem_kv, layer):
    n = mem.shape[0]
    row = lambda w: pl.BlockSpec((ROW_TILE, w), lambda i: (i, 0))
    return pl.pallas_call(
        _mem_kv_kernel,
        grid=(n // ROW_TILE,),
        in_specs=[row(D_MODEL),
                  pl.BlockSpec((None, 1, D_MODEL), lambda i: (layer, 0, 0)),
                  pl.BlockSpec((None, D_MODEL, 2 * X_W), lambda i: (layer, 0, 0))],
        out_specs=[row(X_W), row(X_W)],
        out_shape=[jax.ShapeDtypeStruct((n, X_W), F32), jax.ShapeDtypeStruct((n, X_W), F32)],
        compiler_params=_params(1, 32),
        name="mem_kv",
    )(mem, mem_norm, w_mem_kv)


def _gate_augment(q_ref, k_ref, v_ref, kmt_ref, qa_ref, ka_ref, vt_ref, kt_all_ref, vt_all_ref, sel_ref, nb):
    i = pl.program_id(1)
    blk = MOBA_BLOCK
    qt = q_ref[...].T
    st = jnp.dot(kmt_ref[0], qt, precision=lax.Precision.HIGHEST,
                 preferred_element_type=F32)
    row_block = lax.broadcasted_iota(jnp.int32, st.shape, 0) // MOBA_HEADS
    st = jnp.where(row_block < i, st, -jnp.inf)
    slabs = [st[j * MOBA_HEADS:(j + 1) * MOBA_HEADS] for j in range(nb)]
    for j in range(nb):
        rank = jnp.zeros((MOBA_HEADS, blk), jnp.int32)
        for jp in range(nb):
            if jp == j:
                continue
            ahead = (slabs[jp] >= slabs[j]) if jp < j else (slabs[jp] > slabs[j])
            rank = rank + ahead.astype(jnp.int32)
        limit = jnp.where(j < i, MOBA_TOPK, jnp.where(j == i, nb + 1, 0))
        mask = jnp.where(rank < limit, 0.0, MASKED)
        for c in range(blk // 128):
            sel_ref[c, j * MOBA_HEADS:(j + 1) * MOBA_HEADS, :] = mask[:, c * 128:(c + 1) * 128]

    pad_rows = AUG_W - MOBA_HEAD_DIM - nb
    for h in range(MOBA_HEADS):
        pieces = [qt[h * MOBA_HEAD_DIM:(h + 1) * MOBA_HEAD_DIM] * (LOG2E * MOBA_HEAD_DIM ** -0.5),
                  jnp.concatenate([sel_ref[c, pl.ds(h, nb, stride=MOBA_HEADS), :]
                                   for c in range(blk // 128)], axis=1)]
        if pad_rows:
            pieces.append(jnp.zeros((pad_rows, blk), F32))
        qa_ref[0, h] = jnp.concatenate(pieces, axis=0).astype(BF16)

    lane = lax.broadcasted_iota(jnp.int32, (blk, AUG_W), 1)
    block_col = jnp.where(lane == MOBA_HEAD_DIM + i, 1.0, 0.0)
    k = k_ref[...]
    for pair in range(MOBA_HEADS // 2):
        x = k[:, pair * AUG_W:(pair + 1) * AUG_W]
        ka_ref[0, 2 * pair, 0] = jnp.where(lane < MOBA_HEAD_DIM, x, block_col).astype(BF16)
        ka_ref[0, 2 * pair + 1, 0] = jnp.where(lane < MOBA_HEAD_DIM, pltpu.roll(x, MOBA_HEAD_DIM, 1),
                                               block_col).astype(BF16)
    kt_all_ref[...] = k.T
    vt = v_ref[...].T
    vt_all_ref[...] = vt
    ones_row = jnp.where(lax.broadcasted_iota(jnp.int32, (VT_ROWS - MOBA_HEAD_DIM, blk), 0) == 0, 1.0, 0.0)
    for h in range(MOBA_HEADS):
        vt_ref[0, h, 0] = jnp.concatenate([vt[h * MOBA_HEAD_DIM:(h + 1) * MOBA_HEAD_DIM], ones_row],
                                          axis=0).astype(BF16)


def _moba_kernel(qa_ref, ka_ref, vt_ref, own_ref, prev_ref, o_ref, m_ref, acc_ref):
    i = pl.program_id(1)

    def scores(h, j, bias):
        s = jnp.dot(ka_ref[0, h, j], qa_ref[0, h], preferred_element_type=F32)
        return s if bias is None else s + bias[h]

    def fold(h, j, s, first):
        m_blk = jnp.max(s, axis=0, keepdims=True)
        m_new = m_blk if first else jnp.maximum(m_ref[h], m_blk)
        p = jnp.exp2(s - m_new)
        pv = jnp.dot(vt_ref[0, h, j], p.astype(BF16), preferred_element_type=F32)
        if first:
            acc_ref[h] = pv
        else:
            acc_ref[h] = jnp.exp2(m_ref[h] - m_new) * acc_ref[h] + pv
        m_ref[h] = m_new

    def blocks(*specs):
        items = [(h, j, bias, first) for (j, bias, first) in specs for h in range(MOBA_HEADS)]
        pending = [scores(h, j, bias) for (h, j, bias, _) in items[:LOOKAHEAD]]
        for n, (h, j, _, first) in enumerate(items):
            s = pending.pop(0)
            if n + LOOKAHEAD < len(items):
                nh, nj, nbias, _ = items[n + LOOKAHEAD]
                pending.append(scores(nh, nj, nbias))
            fold(h, j, s, first)

    @pl.when(i == 0)
    def _():
        blocks((i, own_ref, True))

    @pl.when(i >= 1)
    def _():
        blocks((i, own_ref, True), (i - 1, prev_ref, False))

    n_far = jnp.maximum(i - 1, 0)

    def far_run(t, carry):
        blocks(*[(FAR_RUN * t + r, None, False) for r in range(FAR_RUN)])
        return carry

    lax.fori_loop(0, n_far // FAR_RUN, far_run, 0)
    done = (n_far // FAR_RUN) * FAR_RUN
    left = n_far - done

    @pl.when(left >= 2)
    def _():
        blocks((done, None, False), (done + 1, None, False))

    @pl.when(left % 2 == 1)
    def _():
        blocks((n_far - 1, None, False))

    for pair in range(MOBA_HEADS // 2):
        outs = [acc_ref[h, :MOBA_HEAD_DIM] * (1.0 / acc_ref[h, MOBA_HEAD_DIM:MOBA_HEAD_DIM + 1])
                for h in (2 * pair, 2 * pair + 1)]
        o_ref[:, pair * AUG_W:(pair + 1) * AUG_W] = jnp.concatenate(outs, axis=0).T.astype(o_ref.dtype)


def _moba_prompt(qa, ka, vt, bias_own, bias_prev, batch, seq):
    nb = seq // MOBA_BLOCK
    blk = MOBA_BLOCK
    return pl.pallas_call(
        _moba_kernel,
        grid=(batch, nb),
        in_specs=[pl.BlockSpec((1, MOBA_HEADS, AUG_W, blk), lambda b, i: (b, 0, 0, i)),
                  pl.BlockSpec((1, MOBA_HEADS, nb, blk, AUG_W), lambda b, i: (b, 0, 0, 0, 0)),
                  pl.BlockSpec((1, MOBA_HEADS, nb, VT_ROWS, blk), lambda b, i: (b, 0, 0, 0, 0)),
                  pl.BlockSpec((MOBA_HEADS, blk, blk), lambda b, i: (0, 0, 0)),
                  pl.BlockSpec((MOBA_HEADS, blk, blk), lambda b, i: (0, 0, 0))],
        out_specs=pl.BlockSpec((blk, ATT_W), lambda b, i: (b * nb + i, 0)),
        out_shape=jax.ShapeDtypeStruct((batch * seq, ATT_W), BF16),
        scratch_shapes=[pltpu.VMEM((MOBA_HEADS, 1, blk), F32), pltpu.VMEM((MOBA_HEADS, VT_ROWS, blk), F32)],
        compiler_params=_params(2, 48),
        name="moba_prompt",
    )(qa, ka, vt, bias_own, bias_prev)


def _window_means(pad_ref, base, rows, inv_count):
    groups = []
    for g, w in enumerate(POOL_WINDOWS):
        cols = slice(g * POOL_GROUP, (g + 1) * POOL_GROUP)
        tok = pad_ref[base:base + rows, cols]
        acc = tok
        for s in range(1, w):
            acc = acc + pad_ref[base - s:base - s + rows, cols]
        groups.append(acc * inv_count(w) - tok)
    return jnp.concatenate(groups, axis=1)


def _cross_attend(qx, mk, mv):
    outs = []
    for h in range(X_HEADS):
        cols = slice(h * X_HEAD_DIM, (h + 1) * X_HEAD_DIM)
        s = _bdot_nt(qx[:, cols], mk[:, cols]) * (X_HEAD_DIM ** -0.5)
        m = jnp.max(s, axis=-1, keepdims=True)
        p = jnp.exp(s - m)
        l = jnp.sum(p, axis=-1, keepdims=True)
        outs.append(_bdot(p, mv[:, cols]) * (1.0 / l))
    return jnp.concatenate(outs, axis=1)


def _pool_and_cross(u_ref, halo_ref, qx_ref, mk_ref, mv_ref, pooled_ref, c_ref, pad_ref):
    i = pl.program_id(1)
    rows = MOBA_BLOCK
    base = 16
    first = jnp.full((base, 1), i, jnp.int32) == 0
    pad_ref[0:base, :] = jnp.where(first, 0.0, halo_ref[...])
    pad_ref[base:base + rows, :] = u_ref[...]
    pos = i * rows + lax.broadcasted_iota(jnp.int32, (rows, 1), 0)

    def inv_count(w):
        return 1.0 / jnp.minimum(pos + 1, w).astype(F32)

    pooled_ref[...] = _window_means(pad_ref, base, rows, inv_count).astype(pooled_ref.dtype)
    c_ref[...] = _cross_attend(qx_ref[...], mk_ref[...].astype(BF16), mv_ref[...].astype(BF16)).astype(c_ref.dtype)


def _prompt_prep_kernel(*refs, nb, aliased):
    n_in = 11 if aliased else 9
    q_ref, k_ref, v_ref, kmt_ref, u_ref, halo_ref, qx_ref, mk_ref, mv_ref = refs[:9]
    qa_ref, ka_ref, vt_ref, kt_all_ref, vt_all_ref, pooled_ref, c_ref, sel_ref, pad_ref = refs[n_in:]
    _gate_augment(q_ref, k_ref, v_ref, kmt_ref, qa_ref, ka_ref, vt_ref, kt_all_ref, vt_all_ref, sel_ref, nb)
    _pool_and_cross(u_ref, halo_ref, qx_ref, mk_ref, mv_ref, pooled_ref, c_ref, pad_ref)


def _prompt_prep(q, k, v, kmt, u, qx, mk, mv, kv_all, layer, depth, batch, seq, mem_len):
    nb = seq // MOBA_BLOCK
    blk = MOBA_BLOCK
    n = batch * seq
    halo_per_tile = blk // 16
    row = pl.BlockSpec((blk, ATT_W), lambda b, i: (b * nb + i, 0))
    halo = pl.BlockSpec((16, POOL_W), lambda b, i: (jnp.maximum((b * nb + i) * halo_per_tile - 1, 0), 0))
    mem = pl.BlockSpec((mem_len, X_W), lambda b, i: (b, 0))
    slab = pl.BlockSpec((None, None, ATT_W, blk), lambda b, i: (layer, b, 0, i))
    in_specs = [row, row, row, pl.BlockSpec((1, nb * MOBA_HEADS, ATT_W), lambda b, i: (b, 0, 0)),
                row, halo, row, mem, mem]
    args = [q, k, v, kmt, u, u, qx, mk, mv]
    aliases = {}
    if kv_all is not None:
        in_specs += [pl.BlockSpec(memory_space=pl.ANY)] * 2
        args += list(kv_all)
        aliases = {9: 3, 10: 4}
    kv_shape = jax.ShapeDtypeStruct((depth, batch, ATT_W, seq), F32)
    return pl.pallas_call(
        functools.partial(_prompt_prep_kernel, nb=nb, aliased=kv_all is not None),
        grid=(batch, nb),
        in_specs=in_specs,
        out_specs=[pl.BlockSpec((1, MOBA_HEADS, AUG_W, blk), lambda b, i: (b, 0, 0, i)),
                   pl.BlockSpec((1, MOBA_HEADS, 1, blk, AUG_W), lambda b, i: (b, 0, i, 0, 0)),
                   pl.BlockSpec((1, MOBA_HEADS, 1, VT_ROWS, blk), lambda b, i: (b, 0, i, 0, 0)),
                   slab, slab, row, row],
        out_shape=[jax.ShapeDtypeStruct((batch, MOBA_HEADS, AUG_W, seq), BF16),
                   jax.ShapeDtypeStruct((batch, MOBA_HEADS, nb, blk, AUG_W), BF16),
                   jax.ShapeDtypeStruct((batch, MOBA_HEADS, nb, VT_ROWS, blk), BF16),
                   kv_shape, kv_shape,
                   jax.ShapeDtypeStruct((n, POOL_W), BF16), jax.ShapeDtypeStruct((n, X_W), BF16)],
        input_output_aliases=aliases,
        scratch_shapes=[pltpu.VMEM((blk // 128, nb * MOBA_HEADS, 128), F32),
                        pltpu.VMEM((16 + blk, POOL_W), F32)],
        compiler_params=_params(2, 32),
        name="prompt_prep",
    )(*args)


def _sample_kernel(pt_ref, *refs, n_pages, dec_seq):
    seqs = SAMPLE_SEQS
    k_pages = refs[:seqs * n_pages]
    v_pages = refs[seqs * n_pages:2 * seqs * n_pages]
    (q_ref, k_ref, v_ref, u_ref, qx_ref, mk_ref, mv_ref, buf_ref, spast_ref, sown_ref,
     pooled_ref, b_ref, c_ref, pad_ref) = refs[2 * seqs * n_pages:]
    del pt_ref
    toks = seqs * dec_seq
    rows = MOBA_HEADS * toks
    pages_per_block = MOBA_BLOCK // PAGE
    nb = n_pages // pages_per_block
    tok_seq = lax.broadcasted_iota(jnp.int32, (toks, 1), 0) // dec_seq
    col_head = lax.broadcasted_iota(jnp.int32, (toks, ATT_W), 1) // MOBA_HEAD_DIM

    q_tok = q_ref[...] * (MOBA_HEAD_DIM ** -0.5)
    qbd = jnp.concatenate([jnp.where(col_head == h, q_tok, 0.0) for h in range(MOBA_HEADS)], axis=0)
    q_hi = qbd.astype(BF16)
    q_lo = (qbd - q_hi.astype(F32)).astype(BF16)
    q_split = jnp.concatenate([q_hi, q_lo], axis=0)
    k_new = k_ref[...]
    v_new = v_ref[...]
    b_out = jnp.zeros((toks, ATT_W), F32)
    scored = [_sample_scores(q_split, k_pages[a * n_pages:(a + 1) * n_pages], nb) for a in range(seqs)]
    for a in range(seqs):
        b_seq = _sample_attend(*scored[a], qbd, v_pages[a * n_pages:(a + 1) * n_pages],
                               k_new[a * dec_seq:(a + 1) * dec_seq], v_new[a * dec_seq:(a + 1) * dec_seq],
                               spast_ref, sown_ref)
        merged = functools.reduce(
            jnp.add, [jnp.where(col_head == h, b_seq[h * toks:(h + 1) * toks], 0.0) for h in range(MOBA_HEADS)])
        b_out = jnp.where(tok_seq == a, merged, b_out)
    b_ref[...] = b_out

    qx_tok = qx_ref[...] * (X_HEAD_DIM ** -0.5)
    qx_rows = jnp.concatenate([qx_tok[:, h * X_HEAD_DIM:(h + 1) * X_HEAD_DIM] for h in range(X_HEADS)],
                              axis=0)
    mem_rows = mk_ref.shape[1]
    same_head = (lax.broadcasted_iota(jnp.int32, (X_HEADS * toks, mem_rows), 1) % X_HEADS
                 == lax.broadcasted_iota(jnp.int32, (X_HEADS * toks, mem_rows), 0) // toks)
    c_out = jnp.zeros((toks, X_W), F32)
    for a in range(seqs):
        s = jnp.where(same_head, _bdot_nt(qx_rows, mk_ref[a]), MASKED)
        m = jnp.max(s, axis=-1, keepdims=True)
        pr = jnp.exp(s - m)
        l = jnp.sum(pr, axis=-1, keepdims=True)
        o = _bdot(pr, mv_ref[a]) * (1.0 / l)
        merged = jnp.concatenate([o[h * toks:(h + 1) * toks] for h in range(X_HEADS)], axis=1)
        c_out = jnp.where(tok_seq == a, merged, c_out)
    c_ref[...] = c_out

    base = 16
    u_tok = u_ref[...]
    for a in range(seqs):
        pad = pad_ref.at[a]
        pad[base - POOL_STATE:base, :] = buf_ref[a]
        pad[base:base + dec_seq, :] = u_tok[a * dec_seq:(a + 1) * dec_seq]
        pooled_ref[a * dec_seq:(a + 1) * dec_seq, :] = _window_means(pad, base, dec_seq, lambda w: 1.0 / w)


def _sample_scores(q_split, k_pages, nb):
    rows = q_split.shape[0] // 2
    pages_per_block = len(k_pages) // nb
    raw, gate = [], []
    for p, page in enumerate(k_pages):
        s = jnp.dot(q_split, page[...].astype(BF16), preferred_element_type=F32)
        raw.append(s[:rows])
        part = jnp.sum(s[:rows] + s[rows:], axis=-1, keepdims=True)
        if p % pages_per_block == 0:
            gate.append(part)
        else:
            gate[-1] = gate[-1] + part
    return raw, gate


def _sample_attend(raw, gate, qbd, v_pages, k_new, v_new, spast_ref, sown_ref):
    rows = qbd.shape[0]
    n_pages = len(v_pages)
    nb = len(gate)
    dec_seq = k_new.shape[0]
    pages_per_block = n_pages // nb
    logits = []
    for p in range(n_pages):
        j = p // pages_per_block
        rank = jnp.zeros((rows, 1), jnp.int32)
        if p % pages_per_block == 0:
            for jp in range(nb):
                if jp != j:
                    ahead = (gate[jp] >= gate[j]) if jp < j else (gate[jp] > gate[j])
                    rank = rank + ahead.astype(jnp.int32)
            block_mask = jnp.where(rank < MOBA_TOPK, 0.0, MASKED)
        logits.append(raw[p] + spast_ref[:, p * PAGE:(p + 1) * PAGE] + block_mask)
    own = [jnp.sum(qbd * k_new[t:t + 1], axis=-1, keepdims=True) + sown_ref[:, t:t + 1]
           for t in range(dec_seq)]
    m = functools.reduce(jnp.maximum, [jnp.max(s, axis=-1, keepdims=True) for s in logits] + own)
    l = jnp.zeros((rows, 1), F32)
    o = jnp.zeros((rows, ATT_W), F32)
    for p in range(n_pages):
        pr = jnp.exp(logits[p] - m)
        l = l + jnp.sum(pr, axis=-1, keepdims=True)
        o = o + _bdot_nt(pr, v_pages[p][...])
    for t in range(dec_seq):
        pr = jnp.exp(own[t] - m)
        l = l + pr
        o = o + pr * v_new[t:t + 1]
    return o * (1.0 / l)


def _sample_mixers(page_table, cache_k, cache_v, q, k, v, u, qx, mem_k, mem_v, pool_buf, spast, sown, layer,
                   n_prompt):
    dec_batch, n_pages = page_table.shape
    seqs = SAMPLE_SEQS
    dec_seq = (q.shape[0] - n_prompt) // dec_batch
    toks = seqs * dec_seq
    rows = toks * MOBA_HEADS
    mem_rows = mem_k.shape[2]

    def page_spec(a, p):
        return pl.BlockSpec((None, None, ATT_W, PAGE),
                            lambda b, pt: (layer, pt[(b * seqs + a) * n_pages + p], 0, 0))

    pages = [page_spec(a, p) for a in range(seqs) for p in range(n_pages)]
    tok = lambda w: pl.BlockSpec((toks, w), lambda b, pt: (n_prompt // toks + b, 0))
    mem = pl.BlockSpec((None, seqs, mem_rows, X_HEAD_DIM), lambda b, pt: (layer, b, 0, 0))
    in_specs = (pages * 2
                + [tok(ATT_W), tok(ATT_W), tok(ATT_W), tok(POOL_W), tok(X_W), mem, mem,
                   pl.BlockSpec((None, seqs, POOL_STATE, POOL_W), lambda b, pt: (layer, b, 0, 0)),
                   pl.BlockSpec((rows, n_pages * PAGE), lambda b, pt: (0, 0)),
                   pl.BlockSpec((rows, 128), lambda b, pt: (0, 0))])
    out_spec = lambda w: pl.BlockSpec((toks, w), lambda b, pt: (b, 0))
    out = lambda w: jax.ShapeDtypeStruct((dec_batch * dec_seq, w), F32)
    return pl.pallas_call(
        functools.partial(_sample_kernel, n_pages=n_pages, dec_seq=dec_seq),
        grid_spec=pltpu.PrefetchScalarGridSpec(
            num_scalar_prefetch=1,
            grid=(dec_batch // seqs,),
            in_specs=in_specs,
            out_specs=[out_spec(POOL_W), out_spec(ATT_W), out_spec(X_W)],
            scratch_shapes=[pltpu.VMEM((seqs, 16 + 8, POOL_W), F32)]),
        out_shape=[out(POOL_W), out(ATT_W), out(X_W)],
        compiler_params=_params(1, 52),
        name="sample_mixers",
    )(page_table.reshape(-1), *([cache_k] * (seqs * n_pages)), *([cache_v] * (seqs * n_pages)),
      q, k, v, u, qx, mem_k, mem_v, pool_buf, spast, sown)


def _merge_ffn_kernel(x_ref, pooled_p_ref, pooled_s_ref, b_p_ref, b_s_ref, c_p_ref, c_s_ref, gates_ref,
                      pw_ref, ps_ref, wa_ref, wb_ref, wc_ref, wo_ref, g2_ref, wi_ref, wf_ref, o_ref,
                      *, prompt_tiles, d_ff):
    is_sample = jnp.full((MIX_TILE, 1), pl.program_id(0), jnp.int32) >= prompt_tiles
    pooled = jnp.where(is_sample, pooled_s_ref[...].astype(BF16), pooled_p_ref[...])
    b_out = jnp.where(is_sample, b_s_ref[...].astype(BF16), b_p_ref[...])
    c_out = jnp.where(is_sample, c_s_ref[...].astype(BF16), c_p_ref[...])
    mixed = jnp.concatenate(
        [_bdot(pooled[:, g * POOL_GROUP:(g + 1) * POOL_GROUP], pw_ref[g]) for g in range(len(POOL_WINDOWS))],
        axis=1) * ps_ref[...]
    merged = (gates_ref[:, :D_MODEL] * _bdot(mixed, wa_ref[...])
              + gates_ref[:, D_MODEL:2 * D_MODEL] * _bdot(b_out, wb_ref[...])
              + gates_ref[:, 2 * D_MODEL:] * _bdot(c_out, wc_ref[...]))
    x = x_ref[...] + _bdot(merged, wo_ref[...])
    xb = _rmsnorm(x, g2_ref[...]).astype(BF16)
    gate = jnp.dot(xb, wi_ref[:, :d_ff], preferred_element_type=F32)
    up = jnp.dot(xb, wi_ref[:, d_ff:], preferred_element_type=F32)
    act = gate * jax.nn.sigmoid(gate) * up
    o_ref[...] = x + _bdot(act, wf_ref[...])


def _merge_ffn(x, pooled, b_out, c_out, gates, pool_w, pool_scale, w_br_a, w_br_b, w_br_c, w_out,
               norm2, w_ffn_in, w_ffn_out, layer):
    n = x.shape[0]
    d_ff = w_ffn_out.shape[1]
    prompt_tiles = pooled[0].shape[0] // MIX_TILE
    row = lambda w: pl.BlockSpec((MIX_TILE, w), lambda i: (i, 0))
    prompt = lambda w: pl.BlockSpec((MIX_TILE, w), lambda i: (jnp.minimum(i, prompt_tiles - 1), 0))
    sample = lambda w: pl.BlockSpec((MIX_TILE, w), lambda i: (jnp.maximum(i - prompt_tiles, 0), 0))
    wspec = lambda *shape: pl.BlockSpec((None,) + shape, lambda i: (layer,) + (0,) * len(shape),
                                        pipeline_mode=pl.Buffered(1))
    return pl.pallas_call(
        functools.partial(_merge_ffn_kernel, prompt_tiles=prompt_tiles, d_ff=d_ff),
        grid=(n // MIX_TILE,),
        in_specs=[row(D_MODEL), prompt(POOL_W), sample(POOL_W), prompt(ATT_W), sample(ATT_W),
                  prompt(X_W), sample(X_W), row(3 * D_MODEL),
                  wspec(len(POOL_WINDOWS), POOL_GROUP, POOL_GROUP), wspec(1, POOL_W),
                  wspec(POOL_W, D_MODEL), wspec(ATT_W, D_MODEL), wspec(X_W, D_MODEL), wspec(D_MODEL, D_MODEL),
                  wspec(1, D_MODEL), wspec(D_MODEL, 2 * d_ff), wspec(d_ff, D_MODEL)],
        out_specs=row(D_MODEL),
        out_shape=jax.ShapeDtypeStruct((n, D_MODEL), F32),
        compiler_params=_params(1, 58),
        name="merge_ffn",
    )(x, pooled[0], pooled[1], b_out[0], b_out[1], c_out[0], c_out[1], gates,
      pool_w, pool_scale, w_br_a, w_br_b, w_br_c, w_out, norm2, w_ffn_in, w_ffn_out)


def _final_norm_kernel(x_ref, g_ref, o_ref):
    o_ref[...] = _rmsnorm(x_ref[...], g_ref[...])


def _final_norm(x, g, first_row, n_rows):
    first_tile = first_row // ROW_TILE
    return pl.pallas_call(
        _final_norm_kernel,
        grid=(n_rows // ROW_TILE,),
        in_specs=[pl.BlockSpec((ROW_TILE, D_MODEL), lambda i: (first_tile + i, 0)),
                  pl.BlockSpec((1, D_MODEL), lambda i: (0, 0))],
        out_specs=pl.BlockSpec((ROW_TILE, D_MODEL), lambda i: (i, 0)),
        out_shape=jax.ShapeDtypeStruct((n_rows, D_MODEL), F32),
        compiler_params=_params(1, 32),
        name="final_norm",
    )(x, g)


def kernel(x_prompt, x_sample, cache_attn_k, cache_attn_v, cache_mem_k, cache_mem_v, state_pool, page_table, mem_prompt, norm1, w_in, pool_w, pool_scale, mem_norm, w_mem_kv, w_br_a, w_br_b, w_br_c, w_out, norm2, w_ffn_in, w_ffn_out, rel_bias, final_norm):
    batch, seq, d = x_prompt.shape
    dec_batch, dec_seq = x_sample.shape[:2]
    depth = w_in.shape[0]
    n_phys = cache_attn_k.shape[1]
    n_pages = page_table.shape[1]
    mem_len = mem_prompt.shape[1]
    past_len = n_pages * PAGE
    nb = seq // MOBA_BLOCK
    n_prompt = batch * seq
    n_sample = dec_batch * dec_seq
    assert d == D_MODEL and cache_attn_k.shape[2] == PAGE
    assert seq % MIX_TILE == 0 and n_sample % MIX_TILE == 0 and nb % 8 == 0 and nb <= AUG_W - MOBA_HEAD_DIM
    assert past_len % MOBA_BLOCK == 0 and past_len // MOBA_BLOCK >= MOBA_TOPK
    assert SAMPLE_SEQS * dec_seq == 8 and dec_batch % SAMPLE_SEQS == 0
    assert (batch * mem_len) % ROW_TILE == 0 and ROW_TILE % MOBA_BLOCK == 0

    bf = lambda w: w.astype(BF16)
    w_in_b, w_mem_b, pool_w_b = bf(w_in), bf(w_mem_kv), bf(pool_w)
    w_a, w_b, w_c, w_o, w_fi, w_fo = bf(w_br_a), bf(w_br_b), bf(w_br_c), bf(w_out), bf(w_ffn_in), bf(w_ffn_out)
    norm1_r, norm2_r, mem_norm_r = (g.reshape(depth, 1, D_MODEL) for g in (norm1, norm2, mem_norm))
    pool_scale_r = pool_scale.reshape(depth, 1, POOL_W)
    cache_k = cache_attn_k.transpose(0, 1, 3, 4, 2).reshape(depth, n_phys, ATT_W, PAGE)
    cache_v = cache_attn_v.transpose(0, 1, 3, 4, 2).reshape(depth, n_phys, ATT_W, PAGE)
    mem_k_s = cache_mem_k.reshape(depth, dec_batch, mem_len * X_HEADS, X_HEAD_DIM)
    mem_v_s = cache_mem_v.reshape(depth, dec_batch, mem_len * X_HEADS, X_HEAD_DIM)
    mem_rows = mem_prompt.reshape(batch * mem_len, D_MODEL)

    bias_own, bias_prev, spast, sown = _bias_tables(rel_bias, past_len)

    def sample_table(t):
        t = jnp.broadcast_to(t[:, None, :dec_seq], (MOBA_HEADS, SAMPLE_SEQS, dec_seq, t.shape[-1]))
        return t.reshape(MOBA_HEADS * SAMPLE_SEQS * dec_seq, t.shape[-1])

    spast, sown = sample_table(spast), sample_table(sown)
    head_of_col = jnp.arange(ATT_W) // MOBA_HEAD_DIM

    x = jnp.concatenate([x_prompt.reshape(n_prompt, D_MODEL), x_sample.reshape(n_sample, D_MODEL)], axis=0)
    kv_all = None
    mkp_l, mvp_l, bp_l, ks_l, vs_l, bs_l = [], [], [], [], [], []
    for layer in range(depth):
        u, q, k, v, qx, gates, kmean = _in_proj(x, norm1_r, w_in_b, layer)
        mk, mv = _mem_kv(mem_rows, mem_norm_r, w_mem_b, layer)

        km = kmean[:n_prompt // MOBA_BLOCK, 0].reshape(batch, nb, 1, ATT_W)
        kmt = jnp.where(jnp.arange(MOBA_HEADS)[None, None, :, None] == head_of_col[None, None, None, :], km, 0.0)
        kmt = kmt.reshape(batch, nb * MOBA_HEADS, ATT_W)
        qa, ka, vt, kt_all, vt_all, pooled_p, c_prompt = _prompt_prep(
            q, k, v, kmt, u, qx, mk, mv, kv_all, layer, depth, batch, seq, mem_len)
        kv_all = (kt_all, vt_all)
        b_prompt = _moba_prompt(qa, ka, vt, bias_own, bias_prev, batch, seq)

        pooled_s, b_sample, c_sample = _sample_mixers(
            page_table, cache_k, cache_v, q, k, v, u, qx, mem_k_s, mem_v_s, state_pool, spast, sown, layer, n_prompt)

        x = _merge_ffn(x, (pooled_p, pooled_s), (b_prompt, b_sample), (c_prompt, c_sample), gates,
                       pool_w_b, pool_scale_r, w_a, w_b, w_c, w_o, norm2_r, w_fi, w_fo, layer)

        tok = lambda a: a[n_prompt:].reshape(dec_batch, dec_seq, a.shape[-1])
        mkp_l.append(mk.reshape(batch, mem_len, X_HEADS, X_HEAD_DIM))
        mvp_l.append(mv.reshape(batch, mem_len, X_HEADS, X_HEAD_DIM))
        bp_l.append(jnp.stack([u[(b + 1) * seq - POOL_STATE:(b + 1) * seq] for b in range(batch)]))
        ks_l.append(tok(k).reshape(dec_batch, dec_seq, MOBA_HEADS, MOBA_HEAD_DIM))
        vs_l.append(tok(v).reshape(dec_batch, dec_seq, MOBA_HEADS, MOBA_HEAD_DIM))
        bs_l.append(jnp.concatenate([state_pool[layer], tok(u)], axis=1)[:, -POOL_STATE:])

    g_final = final_norm.reshape(1, D_MODEL)
    y_prompt = _final_norm(x, g_final, 0, n_prompt)
    y_sample = _final_norm(x, g_final, n_prompt, n_sample)
    heads_last = lambda t: t.reshape(depth, batch, MOBA_HEADS, MOBA_HEAD_DIM, seq).transpose(0, 1, 4, 2, 3)
    return (y_prompt.reshape(batch, seq, D_MODEL), y_sample.reshape(dec_batch, dec_seq, D_MODEL),
            heads_last(kv_all[0]), heads_last(kv_all[1]), jnp.stack(mkp_l), jnp.stack(mvp_l),
            jnp.stack(bp_l), jnp.stack(ks_l), jnp.stack(vs_l), jnp.stack(bs_l))
```

```python
import functools
import math

import jax
import jax.numpy as jnp
from jax import lax
from jax.experimental import pallas as pl
from jax.experimental.pallas import tpu as pltpu

F32 = jnp.float32
BF16 = jnp.bfloat16

D_MODEL = 1024
POOL_W = 512
POOL_WINDOWS = (2, 4, 8, 16)
POOL_GROUP = POOL_W // len(POOL_WINDOWS)
POOL_STATE = max(POOL_WINDOWS) - 1
MOBA_HEADS = 8
MOBA_HEAD_DIM = 64
ATT_W = MOBA_HEADS * MOBA_HEAD_DIM
MOBA_BLOCK = 256
MOBA_TOPK = 3
X_HEADS = 4
X_HEAD_DIM = 128
X_W = X_HEADS * X_HEAD_DIM
N_BUCKETS = 32
MAX_EXACT = N_BUCKETS // 2
MAX_DISTANCE = 128
EPS = 1e-6
PAGE = 128

ROW_TILE = 256
MIX_TILE = 512
AUG_W = 128
VT_ROWS = 80
LOOKAHEAD = 6
FAR_RUN = 8
SAMPLE_SEQS = 2
MASKED = -1e30
LOG2E = math.log2(math.e)
MIB = 1 << 20


def _params(n_axes, vmem_mib):
    return pltpu.CompilerParams(dimension_semantics=("arbitrary",) * n_axes,
                                vmem_limit_bytes=vmem_mib * MIB)


def _rmsnorm(x, g):
    return x * lax.rsqrt(jnp.mean(x * x, axis=-1, keepdims=True) + EPS) * g


def _bdot(a, b):
    return jnp.dot(a.astype(BF16), b.astype(BF16), preferred_element_type=F32)


def _bdot_nt(a, b):
    return lax.dot_general(a.astype(BF16), b.astype(BF16), (((1,), (1,)), ((), ())),
                           preferred_element_type=F32)


def _bucket(rel):
    n = jnp.maximum(rel, 0)
    nf = jnp.maximum(n, 1).astype(F32)
    large = MAX_EXACT + (jnp.log(nf / MAX_EXACT) / math.log(MAX_DISTANCE / MAX_EXACT)
                         * (N_BUCKETS - MAX_EXACT)).astype(jnp.int32)
    large = jnp.minimum(large, N_BUCKETS - 1)
    return jnp.where(n < MAX_EXACT, n, large)


def _bias_lookup(rel, rb_ref, h):
    bucket = _bucket(rel)
    last = rb_ref[N_BUCKETS - 1, h]
    out = jnp.zeros(rel.shape, F32)
    for b in range(N_BUCKETS - 1):
        out = jnp.where(bucket == b, rb_ref[b, h] - last, out)
    return out


def _bias_table_kernel(rb_ref, own_ref, prev_ref, spast_ref, sown_ref, *, past_len):
    h = pl.program_id(0)
    blk = MOBA_BLOCK
    tk = lax.broadcasted_iota(jnp.int32, (blk, blk), 0)
    tq = lax.broadcasted_iota(jnp.int32, (blk, blk), 1)
    rel = tq - tk
    own_ref[0] = jnp.where(rel >= 0, _bias_lookup(rel, rb_ref, h) * LOG2E, MASKED)
    prev_ref[0] = _bias_lookup(rel + blk, rb_ref, h) * LOG2E
    t = lax.broadcasted_iota(jnp.int32, (8, past_len), 0)
    key = lax.broadcasted_iota(jnp.int32, (8, past_len), 1)
    spast_ref[0] = _bias_lookup(past_len + t - key, rb_ref, h)
    t = lax.broadcasted_iota(jnp.int32, (8, 128), 0)
    key = lax.broadcasted_iota(jnp.int32, (8, 128), 1)
    sown_ref[0] = jnp.where(key <= t, _bias_lookup(t - key, rb_ref, h), MASKED)


def _bias_tables(rel_bias, past_len):
    blk = MOBA_BLOCK
    return pl.pallas_call(
        functools.partial(_bias_table_kernel, past_len=past_len),
        grid=(MOBA_HEADS,),
        in_specs=[pl.BlockSpec(memory_space=pltpu.SMEM)],
        out_specs=[pl.BlockSpec((1, blk, blk), lambda h: (h, 0, 0)),
                   pl.BlockSpec((1, blk, blk), lambda h: (h, 0, 0)),
                   pl.BlockSpec((1, 8, past_len), lambda h: (h, 0, 0)),
                   pl.BlockSpec((1, 8, 128), lambda h: (h, 0, 0))],
        out_shape=[jax.ShapeDtypeStruct((MOBA_HEADS, blk, blk), F32),
                   jax.ShapeDtypeStruct((MOBA_HEADS, blk, blk), F32),
                   jax.ShapeDtypeStruct((MOBA_HEADS, 8, past_len), F32),
                   jax.ShapeDtypeStruct((MOBA_HEADS, 8, 128), F32)],
        compiler_params=_params(1, 32),
        name="bias_tables",
    )(rel_bias)


def _in_proj_kernel(x_ref, g_ref, w_ref, u_ref, q_ref, k_ref, v_ref, qx_ref, gates_ref, kmean_ref):
    xb = _rmsnorm(x_ref[...], g_ref[...]).astype(BF16)

    def proj(lo, n):
        return jnp.dot(xb, w_ref[:, lo:lo + n], preferred_element_type=F32)

    u_ref[...] = proj(0, POOL_W)
    q_ref[...] = proj(POOL_W, ATT_W)
    k = proj(POOL_W + ATT_W, ATT_W)
    k_ref[...] = k
    for r in range(ROW_TILE // MOBA_BLOCK):
        kmean_ref[r] = jnp.mean(k[r * MOBA_BLOCK:(r + 1) * MOBA_BLOCK], axis=0, keepdims=True)
    v_ref[...] = proj(POOL_W + 2 * ATT_W, ATT_W)
    qx_ref[...] = proj(POOL_W + 3 * ATT_W, X_W)
    gates_ref[...] = jax.nn.sigmoid(proj(POOL_W + 3 * ATT_W + X_W, 3 * D_MODEL)).astype(gates_ref.dtype)


def _in_proj(x, norm1, w_in, layer):
    n = x.shape[0]
    in_cols = w_in.shape[-1]
    row = lambda w: pl.BlockSpec((ROW_TILE, w), lambda i: (i, 0))
    blocks_per_tile = ROW_TILE // MOBA_BLOCK
    return pl.pallas_call(
        _in_proj_kernel,
        grid=(n // ROW_TILE,),
        in_specs=[row(D_MODEL),
                  pl.BlockSpec((None, 1, D_MODEL), lambda i: (layer, 0, 0)),
                  pl.BlockSpec((None, D_MODEL, in_cols), lambda i: (layer, 0, 0), pipeline_mode=pl.Buffered(1))],
        out_specs=[row(POOL_W), row(ATT_W), row(ATT_W), row(ATT_W), row(X_W), row(3 * D_MODEL),
                   pl.BlockSpec((blocks_per_tile, 1, ATT_W), lambda i: (i, 0, 0))],
        out_shape=[jax.ShapeDtypeStruct((n, POOL_W), F32), jax.ShapeDtypeStruct((n, ATT_W), F32),
                   jax.ShapeDtypeStruct((n, ATT_W), F32), jax.ShapeDtypeStruct((n, ATT_W), F32),
                   jax.ShapeDtypeStruct((n, X_W), F32), jax.ShapeDtypeStruct((n, 3 * D_MODEL), BF16),
                   jax.ShapeDtypeStruct((n // MOBA_BLOCK, 1, ATT_W), F32)],
        compiler_params=_params(1, 48),
        name="in_proj",
    )(x, norm1, w_in)


def _mem_kv_kernel(x_ref, g_ref, w_ref, mk_ref, mv_ref):
    xb = _rmsnorm(x_ref[...], g_ref[...]).astype(BF16)
    mk_ref[...] = jnp.dot(xb, w_ref[:, :X_W], preferred_element_type=F32)
    mv_ref[...] = jnp.dot(xb, w_ref[:, X_W:], preferred_element_type=F32)


def _mem_kv(mem, mem_norm, w_mem_kv, layer):
    n = mem.shape[0]
    row = lambda w: pl.BlockSpec((ROW_TILE, w), lambda i: (i, 0))
    return pl.pallas_call(
        _mem_kv_kernel,
        grid=(n // ROW_TILE,),
        in_specs=[row(D_MODEL),
                  pl.BlockSpec((None, 1, D_MODEL), lambda i: (layer, 0, 0)),
                  pl.BlockSpec((None, D_MODEL, 2 * X_W), lambda i: (layer, 0, 0))],
        out_specs=[row(X_W), row(X_W)],
        out_shape=[jax.ShapeDtypeStruct((n, X_W), F32), jax.ShapeDtypeStruct((n, X_W), F32)],
        compiler_params=_params(1, 32),
        name="mem_kv",
    )(mem, mem_norm, w_mem_kv)


def _gate_augment(q_ref, k_ref, v_ref, kmt_ref, qa_ref, ka_ref, vt_ref, kt_all_ref, vt_all_ref, sel_ref, nb):
    i = pl.program_id(1)
    blk = MOBA_BLOCK
    qt = q_ref[...].T
    st = jnp.dot(kmt_ref[0], qt, precision=lax.Precision.HIGHEST,
                 preferred_element_type=F32)
    row_block = lax.broadcasted_iota(jnp.int32, st.shape, 0) // MOBA_HEADS
    st = jnp.where(row_block < i, st, -jnp.inf)
    slabs = [st[j * MOBA_HEADS:(j + 1) * MOBA_HEADS] for j in range(nb)]
    for j in range(nb):
        rank = jnp.zeros((MOBA_HEADS, blk), jnp.int32)
        for jp in range(nb):
            if jp == j:
                continue
            ahead = (slabs[jp] >= slabs[j]) if jp < j else (slabs[jp] > slabs[j])
            rank = rank + ahead.astype(jnp.int32)
        limit = jnp.where(j < i, MOBA_TOPK, jnp.where(j == i, nb + 1, 0))
        mask = jnp.where(rank < limit, 0.0, MASKED)
        for c in range(blk // 128):
            sel_ref[c, j * MOBA_HEADS:(j + 1) * MOBA_HEADS, :] = mask[:, c * 128:(c + 1) * 128]

    pad_rows = AUG_W - MOBA_HEAD_DIM - nb
    for h in range(MOBA_HEADS):
        pieces = [qt[h * MOBA_HEAD_DIM:(h + 1) * MOBA_HEAD_DIM] * (LOG2E * MOBA_HEAD_DIM ** -0.5),
                  jnp.concatenate([sel_ref[c, pl.ds(h, nb, stride=MOBA_HEADS), :]
                                   for c in range(blk // 128)], axis=1)]
        if pad_rows:
            pieces.append(jnp.zeros((pad_rows, blk), F32))
        qa_ref[0, h] = jnp.concatenate(pieces, axis=0).astype(BF16)

    lane = lax.broadcasted_iota(jnp.int32, (blk, AUG_W), 1)
    block_col = jnp.where(lane == MOBA_HEAD_DIM + i, 1.0, 0.0)
    k = k_ref[...]
    for pair in range(MOBA_HEADS // 2):
        x = k[:, pair * AUG_W:(pair + 1) * AUG_W]
        ka_ref[0, 2 * pair, 0] = jnp.where(lane < MOBA_HEAD_DIM, x, block_col).astype(BF16)
        ka_ref[0, 2 * pair + 1, 0] = jnp.where(lane < MOBA_HEAD_DIM, pltpu.roll(x, MOBA_HEAD_DIM, 1),
                                               block_col).astype(BF16)
    kt_all_ref[...] = k.T
    vt = v_ref[...].T
    vt_all_ref[...] = vt
    ones_row = jnp.where(lax.broadcasted_iota(jnp.int32, (VT_ROWS - MOBA_HEAD_DIM, blk), 0) == 0, 1.0, 0.0)
    for h in range(MOBA_HEADS):
        vt_ref[0, h, 0] = jnp.concatenate([vt[h * MOBA_HEAD_DIM:(h + 1) * MOBA_HEAD_DIM], ones_row],
                                          axis=0).astype(BF16)


def _moba_kernel(qa_ref, ka_ref, vt_ref, own_ref, prev_ref, o_ref, m_ref, acc_ref):
    i = pl.program_id(1)

    def scores(h, j, bias):
        s = jnp.dot(ka_ref[0, h, j], qa_ref[0, h], preferred_element_type=F32)
        return s if bias is None else s + bias[h]

    def fold(h, j, s, first):
        m_blk = jnp.max(s, axis=0, keepdims=True)
        m_new = m_blk if first else jnp.maximum(m_ref[h], m_blk)
        p = jnp.exp2(s - m_new)
        pv = jnp.dot(vt_ref[0, h, j], p.astype(BF16), preferred_element_type=F32)
        if first:
            acc_ref[h] = pv
        else:
            acc_ref[h] = jnp.exp2(m_ref[h] - m_new) * acc_ref[h] + pv
        m_ref[h] = m_new

    def blocks(*specs):
        items = [(h, j, bias, first) for (j, bias, first) in specs for h in range(MOBA_HEADS)]
        pending = [scores(h, j, bias) for (h, j, bias, _) in items[:LOOKAHEAD]]
        for n, (h, j, _, first) in enumerate(items):
            s = pending.pop(0)
            if n + LOOKAHEAD < len(items):
                nh, nj, nbias, _ = items[n + LOOKAHEAD]
                pending.append(scores(nh, nj, nbias))
            fold(h, j, s, first)

    @pl.when(i == 0)
    def _():
        blocks((i, own_ref, True))

    @pl.when(i >= 1)
    def _():
        blocks((i, own_ref, True), (i - 1, prev_ref, False))

    n_far = jnp.maximum(i - 1, 0)

    def far_run(t, carry):
        blocks(*[(FAR_RUN * t + r, None, False) for r in range(FAR_RUN)])
        return carry

    lax.fori_loop(0, n_far // FAR_RUN, far_run, 0)
    done = (n_far // FAR_RUN) * FAR_RUN
    size = FAR_RUN // 2
    while size >= 1:
        has = ((n_far - done) // size) % 2 == 1
        start = done + ((n_far - done) // (2 * size)) * (2 * size)

        @pl.when(has)
        def _(start=start, size=size):
            blocks(*[(start + r, None, False) for r in range(size)])

        size //= 2

    for pair in range(MOBA_HEADS // 2):
        outs = [acc_ref[h, :MOBA_HEAD_DIM] * (1.0 / acc_ref[h, MOBA_HEAD_DIM:MOBA_HEAD_DIM + 1])
                for h in (2 * pair, 2 * pair + 1)]
        o_ref[:, pair * AUG_W:(pair + 1) * AUG_W] = jnp.concatenate(outs, axis=0).T.astype(o_ref.dtype)


def _moba_prompt(qa, ka, vt, bias_own, bias_prev, batch, seq):
    nb = seq // MOBA_BLOCK
    blk = MOBA_BLOCK
    return pl.pallas_call(
        _moba_kernel,
        grid=(batch, nb),
        in_specs=[pl.BlockSpec((1, MOBA_HEADS, AUG_W, blk), lambda b, i: (b, 0, 0, i)),
                  pl.BlockSpec((1, MOBA_HEADS, nb, blk, AUG_W), lambda b, i: (b, 0, 0, 0, 0)),
                  pl.BlockSpec((1, MOBA_HEADS, nb, VT_ROWS, blk), lambda b, i: (b, 0, 0, 0, 0)),
                  pl.BlockSpec((MOBA_HEADS, blk, blk), lambda b, i: (0, 0, 0)),
                  pl.BlockSpec((MOBA_HEADS, blk, blk), lambda b, i: (0, 0, 0))],
        out_specs=pl.BlockSpec((blk, ATT_W), lambda b, i: (b * nb + i, 0)),
        out_shape=jax.ShapeDtypeStruct((batch * seq, ATT_W), BF16),
        scratch_shapes=[pltpu.VMEM((MOBA_HEADS, 1, blk), F32), pltpu.VMEM((MOBA_HEADS, VT_ROWS, blk), F32)],
        compiler_params=_params(2, 48),
        name="moba_prompt",
    )(qa, ka, vt, bias_own, bias_prev)


def _window_means(pad_ref, base, rows, inv_count):
    groups = []
    for g, w in enumerate(POOL_WINDOWS):
        cols = slice(g * POOL_GROUP, (g + 1) * POOL_GROUP)
        tok = pad_ref[base:base + rows, cols]
        acc = tok
        for s in range(1, w):
            acc = acc + pad_ref[base - s:base - s + rows, cols]
        groups.append(acc * inv_count(w) - tok)
    return jnp.concatenate(groups, axis=1)


def _cross_attend(qx, mk, mv):
    outs = []
    for h in range(X_HEADS):
        cols = slice(h * X_HEAD_DIM, (h + 1) * X_HEAD_DIM)
        s = _bdot_nt(qx[:, cols], mk[:, cols]) * (X_HEAD_DIM ** -0.5)
        m = jnp.max(s, axis=-1, keepdims=True)
        p = jnp.exp(s - m)
        l = jnp.sum(p, axis=-1, keepdims=True)
        outs.append(_bdot(p, mv[:, cols]) * (1.0 / l))
    return jnp.concatenate(outs, axis=1)


def _pool_and_cross(u_ref, halo_ref, qx_ref, mk_ref, mv_ref, pooled_ref, c_ref, pad_ref):
    i = pl.program_id(1)
    rows = MOBA_BLOCK
    base = 16
    first = jnp.full((base, 1), i, jnp.int32) == 0
    pad_ref[0:base, :] = jnp.where(first, 0.0, halo_ref[...])
    pad_ref[base:base + rows, :] = u_ref[...]
    pos = i * rows + lax.broadcasted_iota(jnp.int32, (rows, 1), 0)

    def inv_count(w):
        return 1.0 / jnp.minimum(pos + 1, w).astype(F32)

    pooled_ref[...] = _window_means(pad_ref, base, rows, inv_count).astype(pooled_ref.dtype)
    c_ref[...] = _cross_attend(qx_ref[...], mk_ref[...].astype(BF16), mv_ref[...].astype(BF16)).astype(c_ref.dtype)


def _prompt_prep_kernel(*refs, nb, aliased):
    n_in = 11 if aliased else 9
    q_ref, k_ref, v_ref, kmt_ref, u_ref, halo_ref, qx_ref, mk_ref, mv_ref = refs[:9]
    qa_ref, ka_ref, vt_ref, kt_all_ref, vt_all_ref, pooled_ref, c_ref, sel_ref, pad_ref = refs[n_in:]
    _gate_augment(q_ref, k_ref, v_ref, kmt_ref, qa_ref, ka_ref, vt_ref, kt_all_ref, vt_all_ref, sel_ref, nb)
    _pool_and_cross(u_ref, halo_ref, qx_ref, mk_ref, mv_ref, pooled_ref, c_ref, pad_ref)


def _prompt_prep(q, k, v, kmt, u, qx, mk, mv, kv_all, layer, depth, batch, seq, mem_len):
    nb = seq // MOBA_BLOCK
    blk = MOBA_BLOCK
    n = batch * seq
    halo_per_tile = blk // 16
    row = pl.BlockSpec((blk, ATT_W), lambda b, i: (b * nb + i, 0))
    halo = pl.BlockSpec((16, POOL_W), lambda b, i: (jnp.maximum((b * nb + i) * halo_per_tile - 1, 0), 0))
    mem = pl.BlockSpec((mem_len, X_W), lambda b, i: (b, 0))
    slab = pl.BlockSpec((None, None, ATT_W, blk), lambda b, i: (layer, b, 0, i))
    in_specs = [row, row, row, pl.BlockSpec((1, nb * MOBA_HEADS, ATT_W), lambda b, i: (b, 0, 0)),
                row, halo, row, mem, mem]
    args = [q, k, v, kmt, u, u, qx, mk, mv]
    aliases = {}
    if kv_all is not None:
        in_specs += [pl.BlockSpec(memory_space=pl.ANY)] * 2
        args += list(kv_all)
        aliases = {9: 3, 10: 4}
    kv_shape = jax.ShapeDtypeStruct((depth, batch, ATT_W, seq), F32)
    return pl.pallas_call(
        functools.partial(_prompt_prep_kernel, nb=nb, aliased=kv_all is not None),
        grid=(batch, nb),
        in_specs=in_specs,
        out_specs=[pl.BlockSpec((1, MOBA_HEADS, AUG_W, blk), lambda b, i: (b, 0, 0, i)),
                   pl.BlockSpec((1, MOBA_HEADS, 1, blk, AUG_W), lambda b, i: (b, 0, i, 0, 0)),
                   pl.BlockSpec((1, MOBA_HEADS, 1, VT_ROWS, blk), lambda b, i: (b, 0, i, 0, 0)),
                   slab, slab, row, row],
        out_shape=[jax.ShapeDtypeStruct((batch, MOBA_HEADS, AUG_W, seq), BF16),
                   jax.ShapeDtypeStruct((batch, MOBA_HEADS, nb, blk, AUG_W), BF16),
                   jax.ShapeDtypeStruct((batch, MOBA_HEADS, nb, VT_ROWS, blk), BF16),
                   kv_shape, kv_shape,
                   jax.ShapeDtypeStruct((n, POOL_W), BF16), jax.ShapeDtypeStruct((n, X_W), BF16)],
        input_output_aliases=aliases,
        scratch_shapes=[pltpu.VMEM((blk // 128, nb * MOBA_HEADS, 128), F32),
                        pltpu.VMEM((16 + blk, POOL_W), F32)],
        compiler_params=_params(2, 32),
        name="prompt_prep",
    )(*args)


def _sample_kernel(pt_ref, *refs, n_pages, dec_seq):
    seqs = SAMPLE_SEQS
    k_pages = refs[:seqs * n_pages]
    v_pages = refs[seqs * n_pages:2 * seqs * n_pages]
    (q_ref, k_ref, v_ref, u_ref, qx_ref, mk_ref, mv_ref, buf_ref, spast_ref, sown_ref,
     pooled_ref, b_ref, c_ref, pad_ref) = refs[2 * seqs * n_pages:]
    del pt_ref
    toks = seqs * dec_seq
    rows = MOBA_HEADS * toks
    pages_per_block = MOBA_BLOCK // PAGE
    nb = n_pages // pages_per_block
    tok_seq = lax.broadcasted_iota(jnp.int32, (toks, 1), 0) // dec_seq
    col_head = lax.broadcasted_iota(jnp.int32, (toks, ATT_W), 1) // MOBA_HEAD_DIM

    q_tok = q_ref[...] * (MOBA_HEAD_DIM ** -0.5)
    qbd = jnp.concatenate([jnp.where(col_head == h, q_tok, 0.0) for h in range(MOBA_HEADS)], axis=0)
    q_rows = qbd.astype(BF16)
    k_new = k_ref[...]
    v_new = v_ref[...]
    b_out = jnp.zeros((toks, ATT_W), F32)
    scored = [_sample_scores(q_rows, k_pages[a * n_pages:(a + 1) * n_pages], nb) for a in range(seqs)]
    for a in range(seqs):
        b_seq = _sample_attend(*scored[a], qbd, v_pages[a * n_pages:(a + 1) * n_pages],
                               k_new[a * dec_seq:(a + 1) * dec_seq], v_new[a * dec_seq:(a + 1) * dec_seq],
                               spast_ref, sown_ref)
        merged = functools.reduce(
            jnp.add, [jnp.where(col_head == h, b_seq[h * toks:(h + 1) * toks], 0.0) for h in range(MOBA_HEADS)])
        b_out = jnp.where(tok_seq == a, merged, b_out)
    b_ref[...] = b_out

    qx_tok = qx_ref[...] * (X_HEAD_DIM ** -0.5)
    qx_rows = jnp.concatenate([qx_tok[:, h * X_HEAD_DIM:(h + 1) * X_HEAD_DIM] for h in range(X_HEADS)],
                              axis=0)
    mem_rows = mk_ref.shape[1]
    same_head = (lax.broadcasted_iota(jnp.int32, (X_HEADS * toks, mem_rows), 1) % X_HEADS
                 == lax.broadcasted_iota(jnp.int32, (X_HEADS * toks, mem_rows), 0) // toks)
    c_out = jnp.zeros((toks, X_W), F32)
    for a in range(seqs):
        s = jnp.where(same_head, _bdot_nt(qx_rows, mk_ref[a]), MASKED)
        m = jnp.max(s, axis=-1, keepdims=True)
        pr = jnp.exp(s - m)
        l = jnp.sum(pr, axis=-1, keepdims=True)
        o = _bdot(pr, mv_ref[a]) * (1.0 / l)
        merged = jnp.concatenate([o[h * toks:(h + 1) * toks] for h in range(X_HEADS)], axis=1)
        c_out = jnp.where(tok_seq == a, merged, c_out)
    c_ref[...] = c_out

    base = 16
    u_tok = u_ref[...]
    for a in range(seqs):
        pad = pad_ref.at[a]
        pad[base - POOL_STATE:base, :] = buf_ref[a]
        pad[base:base + dec_seq, :] = u_tok[a * dec_seq:(a + 1) * dec_seq]
        pooled_ref[a * dec_seq:(a + 1) * dec_seq, :] = _window_means(pad, base, dec_seq, lambda w: 1.0 / w)


def _sample_scores(q_rows, k_pages, nb):
    pages_per_block = len(k_pages) // nb
    raw, gate = [], []
    for p, page in enumerate(k_pages):
        s = jnp.dot(q_rows, page[...].astype(BF16), preferred_element_type=F32)
        raw.append(s)
        part = jnp.sum(s, axis=-1, keepdims=True)
        if p % pages_per_block == 0:
            gate.append(part)
        else:
            gate[-1] = gate[-1] + part
    return raw, gate


def _sample_attend(raw, gate, qbd, v_pages, k_new, v_new, spast_ref, sown_ref):
    rows = qbd.shape[0]
    n_pages = len(v_pages)
    nb = len(gate)
    dec_seq = k_new.shape[0]
    pages_per_block = n_pages // nb
    logits = []
    for p in range(n_pages):
        j = p // pages_per_block
        rank = jnp.zeros((rows, 1), jnp.int32)
        if p % pages_per_block == 0:
            for jp in range(nb):
                if jp != j:
                    ahead = (gate[jp] >= gate[j]) if jp < j else (gate[jp] > gate[j])
                    rank = rank + ahead.astype(jnp.int32)
            block_mask = jnp.where(rank < MOBA_TOPK, 0.0, MASKED)
        logits.append(raw[p] + spast_ref[:, p * PAGE:(p + 1) * PAGE] + block_mask)
    own = [jnp.sum(qbd * k_new[t:t + 1], axis=-1, keepdims=True) + sown_ref[:, t:t + 1]
           for t in range(dec_seq)]
    m = functools.reduce(jnp.maximum, [jnp.max(s, axis=-1, keepdims=True) for s in logits] + own)
    l = jnp.zeros((rows, 1), F32)
    o = jnp.zeros((rows, ATT_W), F32)
    for p in range(n_pages):
        pr = jnp.exp(logits[p] - m)
        l = l + jnp.sum(pr, axis=-1, keepdims=True)
        o = o + _bdot_nt(pr, v_pages[p][...])
    for t in range(dec_seq):
        pr = jnp.exp(own[t] - m)
        l = l + pr
        o = o + pr * v_new[t:t + 1]
    return o * (1.0 / l)


def _sample_mixers(page_table, cache_k, cache_v, q, k, v, u, qx, mem_k, mem_v, pool_buf, spast, sown, layer,
                   n_prompt):
    dec_batch, n_pages = page_table.shape
    seqs = SAMPLE_SEQS
    dec_seq = (q.shape[0] - n_prompt) // dec_batch
    toks = seqs * dec_seq
    rows = toks * MOBA_HEADS
    mem_rows = mem_k.shape[2]

    def page_spec(a, p):
        return pl.BlockSpec((None, None, ATT_W, PAGE),
                            lambda b, pt: (layer, pt[(b * seqs + a) * n_pages + p], 0, 0))

    pages = [page_spec(a, p) for a in range(seqs) for p in range(n_pages)]
    tok = lambda w: pl.BlockSpec((toks, w), lambda b, pt: (n_prompt // toks + b, 0))
    mem = pl.BlockSpec((None, seqs, mem_rows, X_HEAD_DIM), lambda b, pt: (layer, b, 0, 0))
    in_specs = (pages * 2
                + [tok(ATT_W), tok(ATT_W), tok(ATT_W), tok(POOL_W), tok(X_W), mem, mem,
                   pl.BlockSpec((None, seqs, POOL_STATE, POOL_W), lambda b, pt: (layer, b, 0, 0)),
                   pl.BlockSpec((rows, n_pages * PAGE), lambda b, pt: (0, 0)),
                   pl.BlockSpec((rows, 128), lambda b, pt: (0, 0))])
    out_spec = lambda w: pl.BlockSpec((toks, w), lambda b, pt: (b, 0))
    out = lambda w: jax.ShapeDtypeStruct((dec_batch * dec_seq, w), F32)
    return pl.pallas_call(
        functools.partial(_sample_kernel, n_pages=n_pages, dec_seq=dec_seq),
        grid_spec=pltpu.PrefetchScalarGridSpec(
            num_scalar_prefetch=1,
            grid=(dec_batch // seqs,),
            in_specs=in_specs,
            out_specs=[out_spec(POOL_W), out_spec(ATT_W), out_spec(X_W)],
            scratch_shapes=[pltpu.VMEM((seqs, 16 + 8, POOL_W), F32)]),
        out_shape=[out(POOL_W), out(ATT_W), out(X_W)],
        compiler_params=_params(1, 52),
        name="sample_mixers",
    )(page_table.reshape(-1), *([cache_k] * (seqs * n_pages)), *([cache_v] * (seqs * n_pages)),
      q, k, v, u, qx, mem_k, mem_v, pool_buf, spast, sown)


def _merge_ffn_kernel(x_ref, pooled_p_ref, pooled_s_ref, b_p_ref, b_s_ref, c_p_ref, c_s_ref, gates_ref,
                      pw_ref, ps_ref, wa_ref, wb_ref, wc_ref, wo_ref, g2_ref, wi_ref, wf_ref, o_ref,
                      *, prompt_tiles, d_ff):
    is_sample = jnp.full((MIX_TILE, 1), pl.program_id(0), jnp.int32) >= prompt_tiles
    pooled = jnp.where(is_sample, pooled_s_ref[...].astype(BF16), pooled_p_ref[...])
    b_out = jnp.where(is_sample, b_s_ref[...].astype(BF16), b_p_ref[...])
    c_out = jnp.where(is_sample, c_s_ref[...].astype(BF16), c_p_ref[...])
    mixed = jnp.concatenate(
        [_bdot(pooled[:, g * POOL_GROUP:(g + 1) * POOL_GROUP], pw_ref[g]) for g in range(len(POOL_WINDOWS))],
        axis=1) * ps_ref[...]
    merged = (gates_ref[:, :D_MODEL] * _bdot(mixed, wa_ref[...])
              + gates_ref[:, D_MODEL:2 * D_MODEL] * _bdot(b_out, wb_ref[...])
              + gates_ref[:, 2 * D_MODEL:] * _bdot(c_out, wc_ref[...]))
    x = x_ref[...] + _bdot(merged, wo_ref[...])
    xb = _rmsnorm(x, g2_ref[...]).astype(BF16)
    gate = jnp.dot(xb, wi_ref[:, :d_ff], preferred_element_type=F32)
    up = jnp.dot(xb, wi_ref[:, d_ff:], preferred_element_type=F32)
    act = gate * jax.nn.sigmoid(gate) * up
    o_ref[...] = x + _bdot(act, wf_ref[...])


def _merge_ffn(x, pooled, b_out, c_out, gates, pool_w, pool_scale, w_br_a, w_br_b, w_br_c, w_out,
               norm2, w_ffn_in, w_ffn_out, layer):
    n = x.shape[0]
    d_ff = w_ffn_out.shape[1]
    prompt_tiles = pooled[0].shape[0] // MIX_TILE
    row = lambda w: pl.BlockSpec((MIX_TILE, w), lambda i: (i, 0))
    prompt = lambda w: pl.BlockSpec((MIX_TILE, w), lambda i: (jnp.minimum(i, prompt_tiles - 1), 0))
    sample = lambda w: pl.BlockSpec((MIX_TILE, w), lambda i: (jnp.maximum(i - prompt_tiles, 0), 0))
    wspec = lambda *shape: pl.BlockSpec((None,) + shape, lambda i: (layer,) + (0,) * len(shape),
                                        pipeline_mode=pl.Buffered(1))
    return pl.pallas_call(
        functools.partial(_merge_ffn_kernel, prompt_tiles=prompt_tiles, d_ff=d_ff),
        grid=(n // MIX_TILE,),
        in_specs=[row(D_MODEL), prompt(POOL_W), sample(POOL_W), prompt(ATT_W), sample(ATT_W),
                  prompt(X_W), sample(X_W), row(3 * D_MODEL),
                  wspec(len(POOL_WINDOWS), POOL_GROUP, POOL_GROUP), wspec(1, POOL_W),
                  wspec(POOL_W, D_MODEL), wspec(ATT_W, D_MODEL), wspec(X_W, D_MODEL), wspec(D_MODEL, D_MODEL),
                  wspec(1, D_MODEL), wspec(D_MODEL, 2 * d_ff), wspec(d_ff, D_MODEL)],
        out_specs=row(D_MODEL),
        out_shape=jax.ShapeDtypeStruct((n, D_MODEL), F32),
        compiler_params=_params(1, 58),
        name="merge_ffn",
    )(x, pooled[0], pooled[1], b_out[0], b_out[1], c_out[0], c_out[1], gates,
      pool_w, pool_scale, w_br_a, w_br_b, w_br_c, w_out, norm2, w_ffn_in, w_ffn_out)


def _final_norm_kernel(x_ref, g_ref, o_ref):
    o_ref[...] = _rmsnorm(x_ref[...], g_ref[...])


def _final_norm(x, g, first_row, n_rows):
    first_tile = first_row // ROW_TILE
    return pl.pallas_call(
        _final_norm_kernel,
        grid=(n_rows // ROW_TILE,),
        in_specs=[pl.BlockSpec((ROW_TILE, D_MODEL), lambda i: (first_tile + i, 0)),
                  pl.BlockSpec((1, D_MODEL), lambda i: (0, 0))],
        out_specs=pl.BlockSpec((ROW_TILE, D_MODEL), lambda i: (i, 0)),
        out_shape=jax.ShapeDtypeStruct((n_rows, D_MODEL), F32),
        compiler_params=_params(1, 32),
        name="final_norm",
    )(x, g)


def kernel(x_prompt, x_sample, cache_attn_k, cache_attn_v, cache_mem_k, cache_mem_v, state_pool, page_table, mem_prompt, norm1, w_in, pool_w, pool_scale, mem_norm, w_mem_kv, w_br_a, w_br_b, w_br_c, w_out, norm2, w_ffn_in, w_ffn_out, rel_bias, final_norm):
    batch, seq, d = x_prompt.shape
    dec_batch, dec_seq = x_sample.shape[:2]
    depth = w_in.shape[0]
    n_phys = cache_attn_k.shape[1]
    n_pages = page_table.shape[1]
    mem_len = mem_prompt.shape[1]
    past_len = n_pages * PAGE
    nb = seq // MOBA_BLOCK
    n_prompt = batch * seq
    n_sample = dec_batch * dec_seq
    assert d == D_MODEL and cache_attn_k.shape[2] == PAGE
    assert seq % MIX_TILE == 0 and n_sample % MIX_TILE == 0 and nb % 8 == 0 and nb <= AUG_W - MOBA_HEAD_DIM
    assert past_len % MOBA_BLOCK == 0 and past_len // MOBA_BLOCK >= MOBA_TOPK
    assert SAMPLE_SEQS * dec_seq == 8 and dec_batch % SAMPLE_SEQS == 0
    assert (batch * mem_len) % ROW_TILE == 0 and ROW_TILE % MOBA_BLOCK == 0

    bf = lambda w: w.astype(BF16)
    w_in_b, w_mem_b, pool_w_b = bf(w_in), bf(w_mem_kv), bf(pool_w)
    w_a, w_b, w_c, w_o, w_fi, w_fo = bf(w_br_a), bf(w_br_b), bf(w_br_c), bf(w_out), bf(w_ffn_in), bf(w_ffn_out)
    norm1_r, norm2_r, mem_norm_r = (g.reshape(depth, 1, D_MODEL) for g in (norm1, norm2, mem_norm))
    pool_scale_r = pool_scale.reshape(depth, 1, POOL_W)
    cache_k = cache_attn_k.transpose(0, 1, 3, 4, 2).reshape(depth, n_phys, ATT_W, PAGE)
    cache_v = cache_attn_v.transpose(0, 1, 3, 4, 2).reshape(depth, n_phys, ATT_W, PAGE)
    mem_k_s = cache_mem_k.reshape(depth, dec_batch, mem_len * X_HEADS, X_HEAD_DIM)
    mem_v_s = cache_mem_v.reshape(depth, dec_batch, mem_len * X_HEADS, X_HEAD_DIM)
    mem_rows = mem_prompt.reshape(batch * mem_len, D_MODEL)

    bias_own, bias_prev, spast, sown = _bias_tables(rel_bias, past_len)

    def sample_table(t):
        t = jnp.broadcast_to(t[:, None, :dec_seq], (MOBA_HEADS, SAMPLE_SEQS, dec_seq, t.shape[-1]))
        return t.reshape(MOBA_HEADS * SAMPLE_SEQS * dec_seq, t.shape[-1])

    spast, sown = sample_table(spast), sample_table(sown)
    head_of_col = jnp.arange(ATT_W) // MOBA_HEAD_DIM

    x = jnp.concatenate([x_prompt.reshape(n_prompt, D_MODEL), x_sample.reshape(n_sample, D_MODEL)], axis=0)
    kv_all = None
    mkp_l, mvp_l, bp_l, ks_l, vs_l, bs_l = [], [], [], [], [], []
    for layer in range(depth):
        u, q, k, v, qx, gates, kmean = _in_proj(x, norm1_r, w_in_b, layer)
        mk, mv = _mem_kv(mem_rows, mem_norm_r, w_mem_b, layer)

        km = kmean[:n_prompt // MOBA_BLOCK, 0].reshape(batch, nb, 1, ATT_W)
        kmt = jnp.where(jnp.arange(MOBA_HEADS)[None, None, :, None] == head_of_col[None, None, None, :], km, 0.0)
        kmt = kmt.reshape(batch, nb * MOBA_HEADS, ATT_W)
        qa, ka, vt, kt_all, vt_all, pooled_p, c_prompt = _prompt_prep(
            q, k, v, kmt, u, qx, mk, mv, kv_all, layer, depth, batch, seq, mem_len)
        kv_all = (kt_all, vt_all)
        b_prompt = _moba_prompt(qa, ka, vt, bias_own, bias_prev, batch, seq)

        pooled_s, b_sample, c_sample = _sample_mixers(
            page_table, cache_k, cache_v, q, k, v, u, qx, mem_k_s, mem_v_s, state_pool, spast, sown, layer, n_prompt)

        x = _merge_ffn(x, (pooled_p, pooled_s), (b_prompt, b_sample), (c_prompt, c_sample), gates,
                       pool_w_b, pool_scale_r, w_a, w_b, w_c, w_o, norm2_r, w_fi, w_fo, layer)

        tok = lambda a: a[n_prompt:].reshape(dec_batch, dec_seq, a.shape[-1])
        mkp_l.append(mk.reshape(batch, mem_len, X_HEADS, X_HEAD_DIM))
        mvp_l.append(mv.reshape(batch, mem_len, X_HEADS, X_HEAD_DIM))
        bp_l.append(jnp.stack([u[(b + 1) * seq - POOL_STATE:(b + 1) * seq] for b in range(batch)]))
        ks_l.append(tok(k).reshape(dec_batch, dec_seq, MOBA_HEADS, MOBA_HEAD_DIM))
        vs_l.append(tok(v).reshape(dec_batch, dec_seq, MOBA_HEADS, MOBA_HEAD_DIM))
        bs_l.append(jnp.concatenate([state_pool[layer], tok(u)], axis=1)[:, -POOL_STATE:])

    g_final = final_norm.reshape(1, D_MODEL)
    y_prompt = _final_norm(x, g_final, 0, n_prompt)
    y_sample = _final_norm(x, g_final, n_prompt, n_sample)
    heads_last = lambda t: t.reshape(depth, batch, MOBA_HEADS, MOBA_HEAD_DIM, seq).transpose(0, 1, 4, 2, 3)
    return (y_prompt.reshape(batch, seq, D_MODEL), y_sample.reshape(dec_batch, dec_seq, D_MODEL),
            heads_last(kv_all[0]), heads_last(kv_all[1]), jnp.stack(mkp_l), jnp.stack(mvp_l),
            jnp.stack(bp_l), jnp.stack(ks_l), jnp.stack(vs_l), jnp.stack(bs_l))
```

```python
import functools
import math

import jax
import jax.numpy as jnp
from jax import lax
from jax.experimental import pallas as pl
from jax.experimental.pallas import tpu as pltpu

F32 = jnp.float32
BF16 = jnp.bfloat16

D_MODEL = 1024
POOL_W = 512
POOL_WINDOWS = (2, 4, 8, 16)
POOL_GROUP = POOL_W // len(POOL_WINDOWS)
POOL_STATE = max(POOL_WINDOWS) - 1
MOBA_HEADS = 8
MOBA_HEAD_DIM = 64
ATT_W = MOBA_HEADS * MOBA_HEAD_DIM
MOBA_BLOCK = 256
MOBA_TOPK = 3
X_HEADS = 4
X_HEAD_DIM = 128
X_W = X_HEADS * X_HEAD_DIM
N_BUCKETS = 32
MAX_EXACT = N_BUCKETS // 2
MAX_DISTANCE = 128
EPS = 1e-6
PAGE = 128

ROW_TILE = 256
MIX_TILE = 512
AUG_W = 128
VT_ROWS = 80
LOOKAHEAD = 6
FAR_RUN = 8
SAMPLE_SEQS = 2
MASKED = -1e30
LOG2E = math.log2(math.e)
MIB = 1 << 20


def _params(n_axes, vmem_mib):
    return pltpu.CompilerParams(dimension_semantics=("arbitrary",) * n_axes,
                                vmem_limit_bytes=vmem_mib * MIB)


def _rmsnorm(x, g):
    return x * lax.rsqrt(jnp.mean(x * x, axis=-1, keepdims=True) + EPS) * g


def _bdot(a, b):
    return jnp.dot(a.astype(BF16), b.astype(BF16), preferred_element_type=F32)


def _bdot_nt(a, b):
    return lax.dot_general(a.astype(BF16), b.astype(BF16), (((1,), (1,)), ((), ())),
                           preferred_element_type=F32)


def _bucket(rel):
    n = jnp.maximum(rel, 0)
    nf = jnp.maximum(n, 1).astype(F32)
    large = MAX_EXACT + (jnp.log(nf / MAX_EXACT) / math.log(MAX_DISTANCE / MAX_EXACT)
                         * (N_BUCKETS - MAX_EXACT)).astype(jnp.int32)
    large = jnp.minimum(large, N_BUCKETS - 1)
    return jnp.where(n < MAX_EXACT, n, large)


def _bias_lookup(rel, rb_ref, h):
    bucket = _bucket(rel)
    last = rb_ref[N_BUCKETS - 1, h]
    out = jnp.zeros(rel.shape, F32)
    for b in range(N_BUCKETS - 1):
        out = jnp.where(bucket == b, rb_ref[b, h] - last, out)
    return out


def _bias_table_kernel(rb_ref, own_ref, prev_ref, spast_ref, sown_ref, *, past_len):
    h = pl.program_id(0)
    blk = MOBA_BLOCK
    tk = lax.broadcasted_iota(jnp.int32, (blk, blk), 0)
    tq = lax.broadcasted_iota(jnp.int32, (blk, blk), 1)
    rel = tq - tk
    own_ref[0] = jnp.where(rel >= 0, _bias_lookup(rel, rb_ref, h) * LOG2E, MASKED)
    prev_ref[0] = _bias_lookup(rel + blk, rb_ref, h) * LOG2E
    t = lax.broadcasted_iota(jnp.int32, (8, past_len), 0)
    key = lax.broadcasted_iota(jnp.int32, (8, past_len), 1)
    spast_ref[0] = _bias_lookup(past_len + t - key, rb_ref, h)
    t = lax.broadcasted_iota(jnp.int32, (8, 128), 0)
    key = lax.broadcasted_iota(jnp.int32, (8, 128), 1)
    sown_ref[0] = jnp.where(key <= t, _bias_lookup(t - key, rb_ref, h), MASKED)


def _bias_tables(rel_bias, past_len):
    blk = MOBA_BLOCK
    return pl.pallas_call(
        functools.partial(_bias_table_kernel, past_len=past_len),
        grid=(MOBA_HEADS,),
        in_specs=[pl.BlockSpec(memory_space=pltpu.SMEM)],
        out_specs=[pl.BlockSpec((1, blk, blk), lambda h: (h, 0, 0)),
                   pl.BlockSpec((1, blk, blk), lambda h: (h, 0, 0)),
                   pl.BlockSpec((1, 8, past_len), lambda h: (h, 0, 0)),
                   pl.BlockSpec((1, 8, 128), lambda h: (h, 0, 0))],
        out_shape=[jax.ShapeDtypeStruct((MOBA_HEADS, blk, blk), F32),
                   jax.ShapeDtypeStruct((MOBA_HEADS, blk, blk), F32),
                   jax.ShapeDtypeStruct((MOBA_HEADS, 8, past_len), F32),
                   jax.ShapeDtypeStruct((MOBA_HEADS, 8, 128), F32)],
        compiler_params=_params(1, 32),
        name="bias_tables",
    )(rel_bias)


def _in_proj_kernel(x_ref, g_ref, w_ref, u_ref, q_ref, k_ref, v_ref, qx_ref, gates_ref, kmean_ref):
    xb = _rmsnorm(x_ref[...], g_ref[...]).astype(BF16)

    def proj(lo, n):
        return jnp.dot(xb, w_ref[:, lo:lo + n], preferred_element_type=F32)

    u_ref[...] = proj(0, POOL_W)
    q_ref[...] = proj(POOL_W, ATT_W)
    k = proj(POOL_W + ATT_W, ATT_W)
    k_ref[...] = k
    for r in range(ROW_TILE // MOBA_BLOCK):
        kmean_ref[r] = jnp.mean(k[r * MOBA_BLOCK:(r + 1) * MOBA_BLOCK], axis=0, keepdims=True)
    v_ref[...] = proj(POOL_W + 2 * ATT_W, ATT_W)
    qx_ref[...] = proj(POOL_W + 3 * ATT_W, X_W)
    gates_ref[...] = jax.nn.sigmoid(proj(POOL_W + 3 * ATT_W + X_W, 3 * D_MODEL)).astype(gates_ref.dtype)


def _in_proj(x, norm1, w_in, layer):
    n = x.shape[0]
    in_cols = w_in.shape[-1]
    row = lambda w: pl.BlockSpec((ROW_TILE, w), lambda i: (i, 0))
    blocks_per_tile = ROW_TILE // MOBA_BLOCK
    return pl.pallas_call(
        _in_proj_kernel,
        grid=(n // ROW_TILE,),
        in_specs=[row(D_MODEL),
                  pl.BlockSpec((None, 1, D_MODEL), lambda i: (layer, 0, 0)),
                  pl.BlockSpec((None, D_MODEL, in_cols), lambda i: (layer, 0, 0), pipeline_mode=pl.Buffered(1))],
        out_specs=[row(POOL_W), row(ATT_W), row(ATT_W), row(ATT_W), row(X_W), row(3 * D_MODEL),
                   pl.BlockSpec((blocks_per_tile, 1, ATT_W), lambda i: (i, 0, 0))],
        out_shape=[jax.ShapeDtypeStruct((n, POOL_W), F32), jax.ShapeDtypeStruct((n, ATT_W), F32),
                   jax.ShapeDtypeStruct((n, ATT_W), F32), jax.ShapeDtypeStruct((n, ATT_W), F32),
                   jax.ShapeDtypeStruct((n, X_W), F32), jax.ShapeDtypeStruct((n, 3 * D_MODEL), BF16),
                   jax.ShapeDtypeStruct((n // MOBA_BLOCK, 1, ATT_W), F32)],
        compiler_params=_params(1, 48),
        name="in_proj",
    )(x, norm1, w_in)


def _mem_kv_kernel(x_ref, g_ref, w_ref, mk_ref, mv_ref):
    xb = _rmsnorm(x_ref[...], g_ref[...]).astype(BF16)
    mk_ref[...] = jnp.dot(xb, w_ref[:, :X_W], preferred_element_type=F32)
    mv_ref[...] = jnp.dot(xb, w_ref[:, X_W:], preferred_element_type=F32)


def _mem_kv(mem, mem_norm, w_mem_kv, layer):
    n = mem.shape[0]
    row = lambda w: pl.BlockSpec((ROW_TILE, w), lambda i: (i, 0))
    return pl.pallas_call(
        _mem_kv_kernel,
        grid=(n // ROW_TILE,),
        in_specs=[row(D_MODEL),
                  pl.BlockSpec((None, 1, D_MODEL), lambda i: (layer, 0, 0)),
                  pl.BlockSpec((None, D_MODEL, 2 * X_W), lambda i: (layer, 0, 0))],
        out_specs=[row(X_W), row(X_W)],
        out_shape=[jax.ShapeDtypeStruct((n, X_W), F32), jax.ShapeDtypeStruct((n, X_W), F32)],
        compiler_params=_params(1, 32),
        name="mem_kv",
    )(mem, mem_norm, w_mem_kv)


def _gate_augment(q_ref, k_ref, v_ref, kmt_ref, qa_ref, ka_ref, vt_ref, kt_all_ref, vt_all_ref, sel_ref, nb):
    i = pl.program_id(1)
    blk = MOBA_BLOCK
    qt = q_ref[...].T
    qt_hi = qt.astype(BF16)
    qt_lo = (qt - qt_hi.astype(F32)).astype(BF16)
    km = kmt_ref[0]
    km_hi = km.astype(BF16)
    km_lo = (km - km_hi.astype(F32)).astype(BF16)
    st = jnp.dot(jnp.concatenate([km_hi, km_hi, km_lo], axis=1), jnp.concatenate([qt_hi, qt_lo, qt_hi], axis=0),
                 preferred_element_type=F32)
    row_block = lax.broadcasted_iota(jnp.int32, st.shape, 0) // MOBA_HEADS
    st = jnp.where(row_block < i, st, -jnp.inf)
    slabs = [st[j * MOBA_HEADS:(j + 1) * MOBA_HEADS] for j in range(nb)]
    for j in range(nb):
        rank = jnp.zeros((MOBA_HEADS, blk), jnp.int32)
        for jp in range(nb):
            if jp == j:
                continue
            ahead = (slabs[jp] >= slabs[j]) if jp < j else (slabs[jp] > slabs[j])
            rank = rank + ahead.astype(jnp.int32)
        limit = jnp.where(j < i, MOBA_TOPK, jnp.where(j == i, nb + 1, 0))
        mask = jnp.where(rank < limit, 0.0, MASKED)
        for c in range(blk // 128):
            sel_ref[c, j * MOBA_HEADS:(j + 1) * MOBA_HEADS, :] = mask[:, c * 128:(c + 1) * 128]

    pad_rows = AUG_W - MOBA_HEAD_DIM - nb
    for h in range(MOBA_HEADS):
        pieces = [qt[h * MOBA_HEAD_DIM:(h + 1) * MOBA_HEAD_DIM] * (LOG2E * MOBA_HEAD_DIM ** -0.5),
                  jnp.concatenate([sel_ref[c, pl.ds(h, nb, stride=MOBA_HEADS), :]
                                   for c in range(blk // 128)], axis=1)]
        if pad_rows:
            pieces.append(jnp.zeros((pad_rows, blk), F32))
        qa_ref[0, h] = jnp.concatenate(pieces, axis=0).astype(BF16)

    lane = lax.broadcasted_iota(jnp.int32, (blk, AUG_W), 1)
    block_col = jnp.where(lane == MOBA_HEAD_DIM + i, 1.0, 0.0)
    k = k_ref[...]
    for pair in range(MOBA_HEADS // 2):
        x = k[:, pair * AUG_W:(pair + 1) * AUG_W]
        ka_ref[0, 2 * pair, 0] = jnp.where(lane < MOBA_HEAD_DIM, x, block_col).astype(BF16)
        ka_ref[0, 2 * pair + 1, 0] = jnp.where(lane < MOBA_HEAD_DIM, pltpu.roll(x, MOBA_HEAD_DIM, 1),
                                               block_col).astype(BF16)
    kt_all_ref[...] = k.T
    vt = v_ref[...].T
    vt_all_ref[...] = vt
    ones_row = jnp.where(lax.broadcasted_iota(jnp.int32, (VT_ROWS - MOBA_HEAD_DIM, blk), 0) == 0, 1.0, 0.0)
    for h in range(MOBA_HEADS):
        vt_ref[0, h, 0] = jnp.concatenate([vt[h * MOBA_HEAD_DIM:(h + 1) * MOBA_HEAD_DIM], ones_row],
                                          axis=0).astype(BF16)


def _moba_kernel(qa_ref, ka_ref, vt_ref, own_ref, prev_ref, o_ref, m_ref, acc_ref):
    i = pl.program_id(1)

    def scores(h, j, bias):
        s = jnp.dot(ka_ref[0, h, j], qa_ref[0, h], preferred_element_type=F32)
        return s if bias is None else s + bias[h]

    def fold(h, j, s, first):
        m_blk = jnp.max(s, axis=0, keepdims=True)
        m_new = m_blk if first else jnp.maximum(m_ref[h], m_blk)
        p = jnp.exp2(s - m_new)
        pv = jnp.dot(vt_ref[0, h, j], p.astype(BF16), preferred_element_type=F32)
        if first:
            acc_ref[h] = pv
        else:
            acc_ref[h] = jnp.exp2(m_ref[h] - m_new) * acc_ref[h] + pv
        m_ref[h] = m_new

    def blocks(*specs):
        items = [(h, j, bias, first) for (j, bias, first) in specs for h in range(MOBA_HEADS)]
        pending = [scores(h, j, bias) for (h, j, bias, _) in items[:LOOKAHEAD]]
        for n, (h, j, _, first) in enumerate(items):
            s = pending.pop(0)
            if n + LOOKAHEAD < len(items):
                nh, nj, nbias, _ = items[n + LOOKAHEAD]
                pending.append(scores(nh, nj, nbias))
            fold(h, j, s, first)

    @pl.when(i == 0)
    def _():
        blocks((i, own_ref, True))

    @pl.when(i >= 1)
    def _():
        blocks((i, own_ref, True), (i - 1, prev_ref, False))

    n_far = jnp.maximum(i - 1, 0)

    def far_run(t, carry):
        blocks(*[(FAR_RUN * t + r, None, False) for r in range(FAR_RUN)])
        return carry

    lax.fori_loop(0, n_far // FAR_RUN, far_run, 0)
    done = (n_far // FAR_RUN) * FAR_RUN
    size = FAR_RUN // 2
    while size >= 1:
        has = ((n_far - done) // size) % 2 == 1
        start = done + ((n_far - done) // (2 * size)) * (2 * size)

        @pl.when(has)
        def _(start=start, size=size):
            blocks(*[(start + r, None, False) for r in range(size)])

        size //= 2

    for pair in range(MOBA_HEADS // 2):
        outs = [acc_ref[h, :MOBA_HEAD_DIM] * (1.0 / acc_ref[h, MOBA_HEAD_DIM:MOBA_HEAD_DIM + 1])
                for h in (2 * pair, 2 * pair + 1)]
        o_ref[:, pair * AUG_W:(pair + 1) * AUG_W] = jnp.concatenate(outs, axis=0).T.astype(o_ref.dtype)


def _moba_prompt(qa, ka, vt, bias_own, bias_prev, batch, seq):
    nb = seq // MOBA_BLOCK
    blk = MOBA_BLOCK
    return pl.pallas_call(
        _moba_kernel,
        grid=(batch, nb),
        in_specs=[pl.BlockSpec((1, MOBA_HEADS, AUG_W, blk), lambda b, i: (b, 0, 0, i)),
                  pl.BlockSpec((1, MOBA_HEADS, nb, blk, AUG_W), lambda b, i: (b, 0, 0, 0, 0)),
                  pl.BlockSpec((1, MOBA_HEADS, nb, VT_ROWS, blk), lambda b, i: (b, 0, 0, 0, 0)),
                  pl.BlockSpec((MOBA_HEADS, blk, blk), lambda b, i: (0, 0, 0)),
                  pl.BlockSpec((MOBA_HEADS, blk, blk), lambda b, i: (0, 0, 0))],
        out_specs=pl.BlockSpec((blk, ATT_W), lambda b, i: (b * nb + i, 0)),
        out_shape=jax.ShapeDtypeStruct((batch * seq, ATT_W), BF16),
        scratch_shapes=[pltpu.VMEM((MOBA_HEADS, 1, blk), F32), pltpu.VMEM((MOBA_HEADS, VT_ROWS, blk), F32)],
        compiler_params=_params(2, 48),
        name="moba_prompt",
    )(qa, ka, vt, bias_own, bias_prev)


def _window_means(pad_ref, base, rows, inv_count):
    groups = []
    for g, w in enumerate(POOL_WINDOWS):
        cols = slice(g * POOL_GROUP, (g + 1) * POOL_GROUP)
        tok = pad_ref[base:base + rows, cols]
        acc = tok
        for s in range(1, w):
            acc = acc + pad_ref[base - s:base - s + rows, cols]
        groups.append(acc * inv_count(w) - tok)
    return jnp.concatenate(groups, axis=1)


def _cross_attend(qx, mk, mv):
    outs = []
    for h in range(X_HEADS):
        cols = slice(h * X_HEAD_DIM, (h + 1) * X_HEAD_DIM)
        s = _bdot_nt(qx[:, cols], mk[:, cols]) * (X_HEAD_DIM ** -0.5)
        m = jnp.max(s, axis=-1, keepdims=True)
        p = jnp.exp(s - m)
        l = jnp.sum(p, axis=-1, keepdims=True)
        outs.append(_bdot(p, mv[:, cols]) * (1.0 / l))
    return jnp.concatenate(outs, axis=1)


def _pool_and_cross(u_ref, halo_ref, qx_ref, mk_ref, mv_ref, pooled_ref, c_ref, pad_ref):
    i = pl.program_id(1)
    rows = MOBA_BLOCK
    base = 16
    first = jnp.full((base, 1), i, jnp.int32) == 0
    pad_ref[0:base, :] = jnp.where(first, 0.0, halo_ref[...])
    pad_ref[base:base + rows, :] = u_ref[...]
    pos = i * rows + lax.broadcasted_iota(jnp.int32, (rows, 1), 0)

    def inv_count(w):
        return 1.0 / jnp.minimum(pos + 1, w).astype(F32)

    pooled_ref[...] = _window_means(pad_ref, base, rows, inv_count).astype(pooled_ref.dtype)
    c_ref[...] = _cross_attend(qx_ref[...], mk_ref[...].astype(BF16), mv_ref[...].astype(BF16)).astype(c_ref.dtype)


def _prompt_prep_kernel(*refs, nb, aliased):
    n_in = 11 if aliased else 9
    q_ref, k_ref, v_ref, kmt_ref, u_ref, halo_ref, qx_ref, mk_ref, mv_ref = refs[:9]
    qa_ref, ka_ref, vt_ref, kt_all_ref, vt_all_ref, pooled_ref, c_ref, sel_ref, pad_ref = refs[n_in:]
    _gate_augment(q_ref, k_ref, v_ref, kmt_ref, qa_ref, ka_ref, vt_ref, kt_all_ref, vt_all_ref, sel_ref, nb)
    _pool_and_cross(u_ref, halo_ref, qx_ref, mk_ref, mv_ref, pooled_ref, c_ref, pad_ref)


def _prompt_prep(q, k, v, kmt, u, qx, mk, mv, kv_all, layer, depth, batch, seq, mem_len):
    nb = seq // MOBA_BLOCK
    blk = MOBA_BLOCK
    n = batch * seq
    halo_per_tile = blk // 16
    row = pl.BlockSpec((blk, ATT_W), lambda b, i: (b * nb + i, 0))
    halo = pl.BlockSpec((16, POOL_W), lambda b, i: (jnp.maximum((b * nb + i) * halo_per_tile - 1, 0), 0))
    mem = pl.BlockSpec((mem_len, X_W), lambda b, i: (b, 0))
    slab = pl.BlockSpec((None, None, ATT_W, blk), lambda b, i: (layer, b, 0, i))
    in_specs = [row, row, row, pl.BlockSpec((1, nb * MOBA_HEADS, ATT_W), lambda b, i: (b, 0, 0)),
                row, halo, row, mem, mem]
    args = [q, k, v, kmt, u, u, qx, mk, mv]
    aliases = {}
    if kv_all is not None:
        in_specs += [pl.BlockSpec(memory_space=pl.ANY)] * 2
        args += list(kv_all)
        aliases = {9: 3, 10: 4}
    kv_shape = jax.ShapeDtypeStruct((depth, batch, ATT_W, seq), F32)
    return pl.pallas_call(
        functools.partial(_prompt_prep_kernel, nb=nb, aliased=kv_all is not None),
        grid=(batch, nb),
        in_specs=in_specs,
        out_specs=[pl.BlockSpec((1, MOBA_HEADS, AUG_W, blk), lambda b, i: (b, 0, 0, i)),
                   pl.BlockSpec((1, MOBA_HEADS, 1, blk, AUG_W), lambda b, i: (b, 0, i, 0, 0)),
                   pl.BlockSpec((1, MOBA_HEADS, 1, VT_ROWS, blk), lambda b, i: (b, 0, i, 0, 0)),
                   slab, slab, row, row],
        out_shape=[jax.ShapeDtypeStruct((batch, MOBA_HEADS, AUG_W, seq), BF16),
                   jax.ShapeDtypeStruct((batch, MOBA_HEADS, nb, blk, AUG_W), BF16),
                   jax.ShapeDtypeStruct((batch, MOBA_HEADS, nb, VT_ROWS, blk), BF16),
                   kv_shape, kv_shape,
                   jax.ShapeDtypeStruct((n, POOL_W), BF16), jax.ShapeDtypeStruct((n, X_W), BF16)],
        input_output_aliases=aliases,
        scratch_shapes=[pltpu.VMEM((blk // 128, nb * MOBA_HEADS, 128), F32),
                        pltpu.VMEM((16 + blk, POOL_W), F32)],
        compiler_params=_params(2, 32),
        name="prompt_prep",
    )(*args)


def _sample_kernel(pt_ref, *refs, n_pages, dec_seq):
    seqs = SAMPLE_SEQS
    k_pages = refs[:seqs * n_pages]
    v_pages = refs[seqs * n_pages:2 * seqs * n_pages]
    (q_ref, k_ref, v_ref, u_ref, qx_ref, mk_ref, mv_ref, buf_ref, spast_ref, sown_ref,
     pooled_ref, b_ref, c_ref, pad_ref) = refs[2 * seqs * n_pages:]
    del pt_ref
    toks = seqs * dec_seq
    rows = MOBA_HEADS * toks
    pages_per_block = MOBA_BLOCK // PAGE
    nb = n_pages // pages_per_block
    tok_seq = lax.broadcasted_iota(jnp.int32, (toks, 1), 0) // dec_seq
    col_head = lax.broadcasted_iota(jnp.int32, (toks, ATT_W), 1) // MOBA_HEAD_DIM

    q_tok = q_ref[...] * (MOBA_HEAD_DIM ** -0.5)
    qbd = jnp.concatenate([jnp.where(col_head == h, q_tok, 0.0) for h in range(MOBA_HEADS)], axis=0)
    q_rows = qbd.astype(BF16)
    k_new = k_ref[...]
    v_new = v_ref[...]
    b_out = jnp.zeros((toks, ATT_W), F32)
    scored = [_sample_scores(q_rows, k_pages[a * n_pages:(a + 1) * n_pages], nb) for a in range(seqs)]
    for a in range(seqs):
        b_seq = _sample_attend(*scored[a], qbd, v_pages[a * n_pages:(a + 1) * n_pages],
                               k_new[a * dec_seq:(a + 1) * dec_seq], v_new[a * dec_seq:(a + 1) * dec_seq],
                               spast_ref, sown_ref)
        merged = functools.reduce(
            jnp.add, [jnp.where(col_head == h, b_seq[h * toks:(h + 1) * toks], 0.0) for h in range(MOBA_HEADS)])
        b_out = jnp.where(tok_seq == a, merged, b_out)
    b_ref[...] = b_out

    qx_tok = qx_ref[...] * (X_HEAD_DIM ** -0.5)
    qx_rows = jnp.concatenate([qx_tok[:, h * X_HEAD_DIM:(h + 1) * X_HEAD_DIM] for h in range(X_HEADS)],
                              axis=0)
    mem_rows = mk_ref.shape[1]
    same_head = (lax.broadcasted_iota(jnp.int32, (X_HEADS * toks, mem_rows), 1) % X_HEADS
                 == lax.broadcasted_iota(jnp.int32, (X_HEADS * toks, mem_rows), 0) // toks)
    c_out = jnp.zeros((toks, X_W), F32)
    for a in range(seqs):
        s = jnp.where(same_head, _bdot_nt(qx_rows, mk_ref[a]), MASKED)
        m = jnp.max(s, axis=-1, keepdims=True)
        pr = jnp.exp(s - m)
        l = jnp.sum(pr, axis=-1, keepdims=True)
        o = _bdot(pr, mv_ref[a]) * (1.0 / l)
        merged = jnp.concatenate([o[h * toks:(h + 1) * toks] for h in range(X_HEADS)], axis=1)
        c_out = jnp.where(tok_seq == a, merged, c_out)
    c_ref[...] = c_out

    base = 16
    u_tok = u_ref[...]
    for a in range(seqs):
        pad = pad_ref.at[a]
        pad[base - POOL_STATE:base, :] = buf_ref[a]
        pad[base:base + dec_seq, :] = u_tok[a * dec_seq:(a + 1) * dec_seq]
        pooled_ref[a * dec_seq:(a + 1) * dec_seq, :] = _window_means(pad, base, dec_seq, lambda w: 1.0 / w)


def _sample_scores(q_rows, k_pages, nb):
    pages_per_block = len(k_pages) // nb
    raw, gate = [], []
    for p, page in enumerate(k_pages):
        s = jnp.dot(q_rows, page[...].astype(BF16), preferred_element_type=F32)
        raw.append(s)
        part = jnp.sum(s, axis=-1, keepdims=True)
        if p % pages_per_block == 0:
            gate.append(part)
        else:
            gate[-1] = gate[-1] + part
    return raw, gate


def _sample_attend(raw, gate, qbd, v_pages, k_new, v_new, spast_ref, sown_ref):
    rows = qbd.shape[0]
    n_pages = len(v_pages)
    nb = len(gate)
    dec_seq = k_new.shape[0]
    pages_per_block = n_pages // nb
    logits = []
    for p in range(n_pages):
        j = p // pages_per_block
        rank = jnp.zeros((rows, 1), jnp.int32)
        if p % pages_per_block == 0:
            for jp in range(nb):
                if jp != j:
                    ahead = (gate[jp] >= gate[j]) if jp < j else (gate[jp] > gate[j])
                    rank = rank + ahead.astype(jnp.int32)
            block_mask = jnp.where(rank < MOBA_TOPK, 0.0, MASKED)
        logits.append(raw[p] + spast_ref[:, p * PAGE:(p + 1) * PAGE] + block_mask)
    own = [jnp.sum(qbd * k_new[t:t + 1], axis=-1, keepdims=True) + sown_ref[:, t:t + 1]
           for t in range(dec_seq)]
    m = functools.reduce(jnp.maximum, [jnp.max(s, axis=-1, keepdims=True) for s in logits] + own)
    l = jnp.zeros((rows, 1), F32)
    o = jnp.zeros((rows, ATT_W), F32)
    for p in range(n_pages):
        pr = jnp.exp(logits[p] - m)
        l = l + jnp.sum(pr, axis=-1, keepdims=True)
        o = o + _bdot_nt(pr, v_pages[p][...])
    for t in range(dec_seq):
        pr = jnp.exp(own[t] - m)
        l = l + pr
        o = o + pr * v_new[t:t + 1]
    return o * (1.0 / l)


def _sample_mixers(page_table, cache_k, cache_v, q, k, v, u, qx, mem_k, mem_v, pool_buf, spast, sown, layer,
                   n_prompt):
    dec_batch, n_pages = page_table.shape
    seqs = SAMPLE_SEQS
    dec_seq = (q.shape[0] - n_prompt) // dec_batch
    toks = seqs * dec_seq
    rows = toks * MOBA_HEADS
    mem_rows = mem_k.shape[2]

    def page_spec(a, p):
        return pl.BlockSpec((None, None, ATT_W, PAGE),
                            lambda b, pt: (layer, pt[(b * seqs + a) * n_pages + p], 0, 0))

    pages = [page_spec(a, p) for a in range(seqs) for p in range(n_pages)]
    tok = lambda w: pl.BlockSpec((toks, w), lambda b, pt: (n_prompt // toks + b, 0))
    mem = pl.BlockSpec((None, seqs, mem_rows, X_HEAD_DIM), lambda b, pt: (layer, b, 0, 0))
    in_specs = (pages * 2
                + [tok(ATT_W), tok(ATT_W), tok(ATT_W), tok(POOL_W), tok(X_W), mem, mem,
                   pl.BlockSpec((None, seqs, POOL_STATE, POOL_W), lambda b, pt: (layer, b, 0, 0)),
                   pl.BlockSpec((rows, n_pages * PAGE), lambda b, pt: (0, 0)),
                   pl.BlockSpec((rows, 128), lambda b, pt: (0, 0))])
    out_spec = lambda w: pl.BlockSpec((toks, w), lambda b, pt: (b, 0))
    out = lambda w: jax.ShapeDtypeStruct((dec_batch * dec_seq, w), F32)
    return pl.pallas_call(
        functools.partial(_sample_kernel, n_pages=n_pages, dec_seq=dec_seq),
        grid_spec=pltpu.PrefetchScalarGridSpec(
            num_scalar_prefetch=1,
            grid=(dec_batch // seqs,),
            in_specs=in_specs,
            out_specs=[out_spec(POOL_W), out_spec(ATT_W), out_spec(X_W)],
            scratch_shapes=[pltpu.VMEM((seqs, 16 + 8, POOL_W), F32)]),
        out_shape=[out(POOL_W), out(ATT_W), out(X_W)],
        compiler_params=_params(1, 52),
        name="sample_mixers",
    )(page_table.reshape(-1), *([cache_k] * (seqs * n_pages)), *([cache_v] * (seqs * n_pages)),
      q, k, v, u, qx, mem_k, mem_v, pool_buf, spast, sown)


def _merge_ffn_kernel(x_ref, pooled_p_ref, pooled_s_ref, b_p_ref, b_s_ref, c_p_ref, c_s_ref, gates_ref,
                      pw_ref, ps_ref, wa_ref, wb_ref, wc_ref, wo_ref, g2_ref, wi_ref, wf_ref, o_ref,
                      *, prompt_tiles, d_ff):
    is_sample = jnp.full((MIX_TILE, 1), pl.program_id(0), jnp.int32) >= prompt_tiles
    pooled = jnp.where(is_sample, pooled_s_ref[...].astype(BF16), pooled_p_ref[...])
    b_out = jnp.where(is_sample, b_s_ref[...].astype(BF16), b_p_ref[...])
    c_out = jnp.where(is_sample, c_s_ref[...].astype(BF16), c_p_ref[...])
    mixed = jnp.concatenate(
        [_bdot(pooled[:, g * POOL_GROUP:(g + 1) * POOL_GROUP], pw_ref[g]) for g in range(len(POOL_WINDOWS))],
        axis=1) * ps_ref[...]
    merged = (gates_ref[:, :D_MODEL] * _bdot(mixed, wa_ref[...])
              + gates_ref[:, D_MODEL:2 * D_MODEL] * _bdot(b_out, wb_ref[...])
              + gates_ref[:, 2 * D_MODEL:] * _bdot(c_out, wc_ref[...]))
    x = x_ref[...] + _bdot(merged, wo_ref[...])
    xb = _rmsnorm(x, g2_ref[...]).astype(BF16)
    gate = jnp.dot(xb, wi_ref[:, :d_ff], preferred_element_type=F32)
    up = jnp.dot(xb, wi_ref[:, d_ff:], preferred_element_type=F32)
    act = gate * jax.nn.sigmoid(gate) * up
    o_ref[...] = x + _bdot(act, wf_ref[...])


def _merge_ffn(x, pooled, b_out, c_out, gates, pool_w, pool_scale, w_br_a, w_br_b, w_br_c, w_out,
               norm2, w_ffn_in, w_ffn_out, layer):
    n = x.shape[0]
    d_ff = w_ffn_out.shape[1]
    prompt_tiles = pooled[0].shape[0] // MIX_TILE
    row = lambda w: pl.BlockSpec((MIX_TILE, w), lambda i: (i, 0))
    prompt = lambda w: pl.BlockSpec((MIX_TILE, w), lambda i: (jnp.minimum(i, prompt_tiles - 1), 0))
    sample = lambda w: pl.BlockSpec((MIX_TILE, w), lambda i: (jnp.maximum(i - prompt_tiles, 0), 0))
    wspec = lambda *shape: pl.BlockSpec((None,) + shape, lambda i: (layer,) + (0,) * len(shape),
                                        pipeline_mode=pl.Buffered(1))
    return pl.pallas_call(
        functools.partial(_merge_ffn_kernel, prompt_tiles=prompt_tiles, d_ff=d_ff),
        grid=(n // MIX_TILE,),
        in_specs=[row(D_MODEL), prompt(POOL_W), sample(POOL_W), prompt(ATT_W), sample(ATT_W),
                  prompt(X_W), sample(X_W), row(3 * D_MODEL),
                  wspec(len(POOL_WINDOWS), POOL_GROUP, POOL_GROUP), wspec(1, POOL_W),
                  wspec(POOL_W, D_MODEL), wspec(ATT_W, D_MODEL), wspec(X_W, D_MODEL), wspec(D_MODEL, D_MODEL),
                  wspec(1, D_MODEL), wspec(D_MODEL, 2 * d_ff), wspec(d_ff, D_MODEL)],
        out_specs=row(D_MODEL),
        out_shape=jax.ShapeDtypeStruct((n, D_MODEL), F32),
        compiler_params=_params(1, 58),
        name="merge_ffn",
    )(x, pooled[0], pooled[1], b_out[0], b_out[1], c_out[0], c_out[1], gates,
      pool_w, pool_scale, w_br_a, w_br_b, w_br_c, w_out, norm2, w_ffn_in, w_ffn_out)


def _final_norm_kernel(x_ref, g_ref, o_ref):
    o_ref[...] = _rmsnorm(x_ref[...], g_ref[...])


def _final_norm(x, g, first_row, n_rows):
    first_tile = first_row // MIX_TILE
    return pl.pallas_call(
        _final_norm_kernel,
        grid=(n_rows // MIX_TILE,),
        in_specs=[pl.BlockSpec((MIX_TILE, D_MODEL), lambda i: (first_tile + i, 0)),
                  pl.BlockSpec((1, D_MODEL), lambda i: (0, 0))],
        out_specs=pl.BlockSpec((MIX_TILE, D_MODEL), lambda i: (i, 0)),
        out_shape=jax.ShapeDtypeStruct((n_rows, D_MODEL), F32),
        compiler_params=_params(1, 32),
        name="final_norm",
    )(x, g)


def kernel(x_prompt, x_sample, cache_attn_k, cache_attn_v, cache_mem_k, cache_mem_v, state_pool, page_table, mem_prompt, norm1, w_in, pool_w, pool_scale, mem_norm, w_mem_kv, w_br_a, w_br_b, w_br_c, w_out, norm2, w_ffn_in, w_ffn_out, rel_bias, final_norm):
    batch, seq, d = x_prompt.shape
    dec_batch, dec_seq = x_sample.shape[:2]
    depth = w_in.shape[0]
    n_phys = cache_attn_k.shape[1]
    n_pages = page_table.shape[1]
    mem_len = mem_prompt.shape[1]
    past_len = n_pages * PAGE
    nb = seq // MOBA_BLOCK
    n_prompt = batch * seq
    n_sample = dec_batch * dec_seq
    assert d == D_MODEL and cache_attn_k.shape[2] == PAGE
    assert seq % MIX_TILE == 0 and n_sample % MIX_TILE == 0 and nb % 8 == 0 and nb <= AUG_W - MOBA_HEAD_DIM
    assert past_len % MOBA_BLOCK == 0 and past_len // MOBA_BLOCK >= MOBA_TOPK
    assert SAMPLE_SEQS * dec_seq == 8 and dec_batch % SAMPLE_SEQS == 0
    assert (batch * mem_len) % ROW_TILE == 0 and ROW_TILE % MOBA_BLOCK == 0

    bf = lambda w: w.astype(BF16)
    w_in_b, w_mem_b, pool_w_b = bf(w_in), bf(w_mem_kv), bf(pool_w)
    w_a, w_b, w_c, w_o, w_fi, w_fo = bf(w_br_a), bf(w_br_b), bf(w_br_c), bf(w_out), bf(w_ffn_in), bf(w_ffn_out)
    norm1_r, norm2_r, mem_norm_r = (g.reshape(depth, 1, D_MODEL) for g in (norm1, norm2, mem_norm))
    pool_scale_r = pool_scale.reshape(depth, 1, POOL_W)
    cache_k = cache_attn_k.transpose(0, 1, 3, 4, 2).reshape(depth, n_phys, ATT_W, PAGE)
    cache_v = cache_attn_v.transpose(0, 1, 3, 4, 2).reshape(depth, n_phys, ATT_W, PAGE)
    mem_k_s = cache_mem_k.reshape(depth, dec_batch, mem_len * X_HEADS, X_HEAD_DIM)
    mem_v_s = cache_mem_v.reshape(depth, dec_batch, mem_len * X_HEADS, X_HEAD_DIM)
    mem_rows = mem_prompt.reshape(batch * mem_len, D_MODEL)

    bias_own, bias_prev, spast, sown = _bias_tables(rel_bias, past_len)

    def sample_table(t):
        t = jnp.broadcast_to(t[:, None, :dec_seq], (MOBA_HEADS, SAMPLE_SEQS, dec_seq, t.shape[-1]))
        return t.reshape(MOBA_HEADS * SAMPLE_SEQS * dec_seq, t.shape[-1])

    spast, sown = sample_table(spast), sample_table(sown)
    head_of_col = jnp.arange(ATT_W) // MOBA_HEAD_DIM

    x = jnp.concatenate([x_prompt.reshape(n_prompt, D_MODEL), x_sample.reshape(n_sample, D_MODEL)], axis=0)
    kv_all = None
    mkp_l, mvp_l, bp_l, ks_l, vs_l, bs_l = [], [], [], [], [], []
    for layer in range(depth):
        u, q, k, v, qx, gates, kmean = _in_proj(x, norm1_r, w_in_b, layer)
        mk, mv = _mem_kv(mem_rows, mem_norm_r, w_mem_b, layer)

        km = kmean[:n_prompt // MOBA_BLOCK, 0].reshape(batch, nb, 1, ATT_W)
        kmt = jnp.where(jnp.arange(MOBA_HEADS)[None, None, :, None] == head_of_col[None, None, None, :], km, 0.0)
        kmt = kmt.reshape(batch, nb * MOBA_HEADS, ATT_W)
        qa, ka, vt, kt_all, vt_all, pooled_p, c_prompt = _prompt_prep(
            q, k, v, kmt, u, qx, mk, mv, kv_all, layer, depth, batch, seq, mem_len)
        kv_all = (kt_all, vt_all)
        b_prompt = _moba_prompt(qa, ka, vt, bias_own, bias_prev, batch, seq)

        pooled_s, b_sample, c_sample = _sample_mixers(
            page_table, cache_k, cache_v, q, k, v, u, qx, mem_k_s, mem_v_s, state_pool, spast, sown, layer, n_prompt)

        x = _merge_ffn(x, (pooled_p, pooled_s), (b_prompt, b_sample), (c_prompt, c_sample), gates,
                       pool_w_b, pool_scale_r, w_a, w_b, w_c, w_o, norm2_r, w_fi, w_fo, layer)

        tok = lambda a: a[n_prompt:].reshape(dec_batch, dec_seq, a.shape[-1])
        mkp_l.append(mk.reshape(batch, mem_len, X_HEADS, X_HEAD_DIM))
        mvp_l.append(mv.reshape(batch, mem_len, X_HEADS, X_HEAD_DIM))
        bp_l.append(jnp.stack([u[(b + 1) * seq - POOL_STATE:(b + 1) * seq] for b in range(batch)]))
        ks_l.append(tok(k).reshape(dec_batch, dec_seq, MOBA_HEADS, MOBA_HEAD_DIM))
        vs_l.append(tok(v).reshape(dec_batch, dec_seq, MOBA_HEADS, MOBA_HEAD_DIM))
        bs_l.append(jnp.concatenate([state_pool[layer], tok(u)], axis=1)[:, -POOL_STATE:])

    g_final = final_norm.reshape(1, D_MODEL)
    y_prompt = _final_norm(x, g_final, 0, n_prompt)
    y_sample = _final_norm(x, g_final, n_prompt, n_sample)
    heads_last = lambda t: t.reshape(depth, batch, MOBA_HEADS, MOBA_HEAD_DIM, seq).transpose(0, 1, 4, 2, 3)
    return (y_prompt.reshape(batch, seq, D_MODEL), y_sample.reshape(dec_batch, dec_seq, D_MODEL),
            heads_last(kv_all[0]), heads_last(kv_all[1]), jnp.stack(mkp_l), jnp.stack(mvp_l),
            jnp.stack(bp_l), jnp.stack(ks_l), jnp.stack(vs_l), jnp.stack(bs_l))
```
